```python
import math
import jax
import jax.numpy as jnp
from jax import lax
import numpy as np

D_MODEL = 2048
BATCH = 2
SEQ = 8192
DEPTH = 2
DEC_BATCH = 16
DEC_SEQ = 16
PAST_LEN = 2048

CHUNK = 64
QBLOCK = 128
N_A_LAYERS = DEPTH // 2
N_B_LAYERS = DEPTH - N_A_LAYERS
GDN_HEAD_K = 128
GDN_HEAD_V = 128
GDN_K_HEADS = D_MODEL // 128
GDN_V_HEADS = 2 * GDN_K_HEADS
GDN_QK_W = GDN_K_HEADS * GDN_HEAD_K
GDN_V_W = GDN_V_HEADS * GDN_HEAD_V
GDN_QKV_W = 2 * GDN_QK_W + GDN_V_W
GDN_IN_W = GDN_QKV_W + GDN_V_W + 2 * GDN_V_HEADS
GDN_CONV = 4
DIFF_HEAD_DIM = 128
DIFF_HEADS = D_MODEL // 256
DIFF_QK_W = 2 * DIFF_HEADS * DIFF_HEAD_DIM
DIFF_V_W = DIFF_HEADS * 2 * DIFF_HEAD_DIM
DIFF_SUBLN_EPS = 1e-5
NUM_BUCKETS = 32
MAX_DISTANCE = 128
D_FF = 11 * D_MODEL // 4
FFN_CONV = 3
NORM_EPS = 1e-6
NEG_INF = -1e30

kernel_name = 'streaming_gdn_diffattn_yoco'


def _rmsnorm(x, g, eps=NORM_EPS):
    xf = x.astype(jnp.float32)
    y = xf * lax.rsqrt(jnp.mean(xf * xf, axis=-1, keepdims=True) + eps)
    return (y * g.astype(jnp.float32)).astype(x.dtype)


def _l2norm(x):
    return x * lax.rsqrt(jnp.sum(x * x, axis=-1, keepdims=True) + 1e-6)


def _causal_dwconv(x, w, buf):
    L = x.shape[1]
    width = w.shape[0]
    xp = jnp.concatenate([buf.astype(x.dtype), x], axis=1)
    y = xp[:, 0:L] * w[0]
    for i in range(1, width):
        y = y + xp[:, i:i + L] * w[i]
    return y, xp[:, xp.shape[1] - (width - 1):]


def _gated_delta_rule(q, k, v, g, beta, S0):
    B, L, H, DK = q.shape
    DV = v.shape[-1]
    c = min(CHUNK, L)
    n = L // c

    def blk(t):
        return jnp.moveaxis(t.reshape((B, n, c, H) + t.shape[3:]), 3, 2)

    q, k, v, g, beta = blk(q), blk(k), blk(v), blk(g), blk(beta)
    G = jnp.cumsum(g, axis=-1)
    idx = jnp.arange(c)
    incl = idx[:, None] >= idx[None, :]
    strict = idx[:, None] > idx[None, :]
    dec_incl = jnp.exp(jnp.where(incl, G[..., :, None] - G[..., None, :], -jnp.inf))
    dec_strict = jnp.where(strict, dec_incl, 0.0)
    A = beta[..., :, None] * jnp.einsum('bnhid,bnhjd->bnhij', k, k) * dec_strict
    eye = jnp.eye(c, dtype=jnp.float32)
    rhs = jnp.concatenate([beta[..., None] * v, (beta * jnp.exp(G))[..., None] * k], axis=-1)
    sol = lax.linalg.triangular_solve(A + eye, rhs, left_side=True, lower=True, unit_diagonal=True)
    u_v, w = sol[..., :DV], sol[..., DV:]
    P = jnp.einsum('bnhid,bnhjd->bnhij', q, k) * dec_incl
    q_g = q * jnp.exp(G)[..., None]
    k_d = k * jnp.exp(G[..., -1:] - G)[..., None]
    g_last = jnp.exp(G[..., -1])

    def step(S, xs):
        u_v_c, w_c, P_c, q_g_c, k_d_c, g_last_c = xs
        u = u_v_c - jnp.einsum('bhck,bhkv->bhcv', w_c, S)
        o = jnp.einsum('bhck,bhkv->bhcv', q_g_c, S) + jnp.einsum('bhij,bhjv->bhiv', P_c, u)
        S = g_last_c[..., None, None] * S + jnp.einsum('bhck,bhcv->bhkv', k_d_c, u)
        return S, o

    xs = tuple(jnp.moveaxis(t, 1, 0) for t in (u_v, w, P, q_g, k_d, g_last))
    S, o = lax.scan(step, S0, xs)
    o = jnp.transpose(o, (1, 0, 3, 2, 4)).reshape(B, L, H, DV)
    return o, S


def _gdn_mixer(xn, S0, conv_buf, w_in, w_conv, a_log, dt_bias, out_norm, w_out):
    B, L, _ = xn.shape
    proj = xn @ w_in
    o0 = GDN_QKV_W
    o1 = o0 + GDN_V_W
    o2 = o1 + GDN_V_HEADS
    qkv, z, b, a = proj[..., :o0], proj[..., o0:o1], proj[..., o1:o2], proj[..., o2:]
    qkv, new_buf = _causal_dwconv(qkv, w_conv, conv_buf)
    qkv = jax.nn.silu(qkv).astype(jnp.float32)
    rep = GDN_V_HEADS // GDN_K_HEADS
    q = _l2norm(qkv[..., :GDN_QK_W].reshape(B, L, GDN_K_HEADS, GDN_HEAD_K)) * (GDN_HEAD_K ** -0.5)
    k = _l2norm(qkv[..., GDN_QK_W:2 * GDN_QK_W].reshape(B, L, GDN_K_HEADS, GDN_HEAD_K))
    q = jnp.repeat(q, rep, axis=2)
    k = jnp.repeat(k, rep, axis=2)
    v = qkv[..., 2 * GDN_QK_W:].reshape(B, L, GDN_V_HEADS, GDN_HEAD_V)
    beta = jax.nn.sigmoid(b.astype(jnp.float32))
    g = -jnp.exp(a_log.astype(jnp.float32)) * jax.nn.softplus(a.astype(jnp.float32) + dt_bias.astype(jnp.float32))
    o, S = _gated_delta_rule(q, k, v, g, beta, S0.astype(jnp.float32))
    o = _rmsnorm(o, out_norm) * jax.nn.silu(z.astype(jnp.float32)).reshape(B, L, GDN_V_HEADS, GDN_HEAD_V)
    y = o.reshape(B, L, GDN_V_W).astype(xn.dtype) @ w_out
    return y, S.astype(xn.dtype), new_buf


def _conv_ffn(xn, buf, w_up, w_conv, b_conv, w_down):
    u, new_buf = _causal_dwconv(xn @ w_up, w_conv, buf)
    u = u + b_conv
    gate, val = u[..., :D_FF], u[..., D_FF:]
    return (jax.nn.silu(gate) * val) @ w_down, new_buf


def _t5_bucket(rel):
    half = NUM_BUCKETS // 2
    exact = half // 2
    n = jnp.abs(rel)
    large = exact + (jnp.log(jnp.maximum(n, 1).astype(jnp.float32) / exact)
                     / math.log(MAX_DISTANCE / exact) * (half - exact)).astype(jnp.int32)
    large = jnp.minimum(large, half - 1)
    return jnp.where(rel > 0, half, 0) + jnp.where(n < exact, n, large)


def _diff_core(q, k, v, q_pos, k_pos, rel_bias, lam):
    logits = jnp.einsum('bqhmd,bkhmd->bhmqk', q, k, preferred_element_type=jnp.float32) * (DIFF_HEAD_DIM ** -0.5)
    bias = rel_bias[_t5_bucket(k_pos[None, :] - q_pos[:, None])].astype(jnp.float32)
    bias = jnp.transpose(bias, (2, 0, 1))[None, :, None]
    visible = (k_pos[None, :] // CHUNK) <= (q_pos[:, None] // CHUNK)
    p = jax.nn.softmax(jnp.where(visible, logits + bias, NEG_INF), axis=-1)
    wts = p[:, :, 0] - lam * p[:, :, 1]
    return jnp.einsum('bhqk,bkhe->bqhe', wts.astype(v.dtype), v)


def _attend_blocks(q, k, v, q_pos, k_pos, rel_bias, lam):
    B, L = q.shape[:2]
    nb = L // QBLOCK
    qb = jnp.swapaxes(q.reshape((B, nb, QBLOCK) + q.shape[2:]), 0, 1)
    pb = q_pos.reshape(nb, QBLOCK)
    ob = lax.map(lambda t: _diff_core(t[0], k, v, t[1], k_pos, rel_bias, lam), (qb, pb))
    return jnp.swapaxes(ob, 0, 1).reshape((B, L) + ob.shape[3:])


def _diff_mixer(xn, k_all, v_all, q_pos, k_pos, w_q, lq1, lk1, lq2, lk2, sub_norm, w_o, rel_bias, lam_init, sweep):
    B, L, _ = xn.shape
    q = (xn @ w_q).reshape(B, L, DIFF_HEADS, 2, DIFF_HEAD_DIM)
    k = k_all.reshape(B, k_all.shape[1], DIFF_HEADS, 2, DIFF_HEAD_DIM)
    f32 = jnp.float32
    lam = (jnp.exp(jnp.sum(lq1.astype(f32) * lk1.astype(f32)))
           - jnp.exp(jnp.sum(lq2.astype(f32) * lk2.astype(f32))) + lam_init)
    if sweep:
        o = _attend_blocks(q, k, v_all, q_pos, k_pos, rel_bias, lam)
    else:
        o = _diff_core(q, k, v_all, q_pos, k_pos, rel_bias, lam)
    o = _rmsnorm(o, sub_norm, eps=DIFF_SUBLN_EPS) * (1.0 - lam_init)
    return o.reshape(B, L, DIFF_V_W) @ w_o


def _trunk(x, pos0, st_gdn, st_gdn_conv, st_ffn_conv, cache_k, cache_v, p):
    B, L, _ = x.shape
    fresh = cache_k is None
    q_pos = pos0 + jnp.arange(L, dtype=jnp.int32)
    new_S, new_gconv, new_fconv = [], [], []
    k_sh = None
    v_sh = None
    h = x
    for layer in range(DEPTH):
        if layer < N_A_LAYERS:
            i = layer
            S0 = jnp.zeros((B, GDN_V_HEADS, GDN_HEAD_K, GDN_HEAD_V), x.dtype) if fresh else st_gdn[i]
            cb = jnp.zeros((B, GDN_CONV - 1, GDN_QKV_W), x.dtype) if fresh else st_gdn_conv[i]
            o, S, cb_new = _gdn_mixer(_rmsnorm(h, p['a_norm'][i]), S0, cb, p['a_w_in'][i], p['a_w_conv'][i],
                                      p['a_log'][i], p['a_dt_bias'][i], p['a_out_norm'][i], p['a_w_out'][i])
            h = h + o
            new_S.append(S)
            new_gconv.append(cb_new)
        else:
            j = layer - N_A_LAYERS
            lam_init = 0.8 - 0.6 * math.exp(-0.3 * layer)
            if fresh:
                k_all, v_all, k_pos = k_sh, v_sh, q_pos
            else:
                k_all = jnp.concatenate([cache_k, k_sh], axis=1)
                v_all = jnp.concatenate([cache_v, v_sh], axis=1)
                k_pos = jnp.arange(cache_k.shape[1] + L, dtype=jnp.int32)
            o = _diff_mixer(_rmsnorm(h, p['b_norm'][j]), k_all, v_all, q_pos, k_pos, p['b_w_q'][j],
                            p['b_lam_q1'][j], p['b_lam_k1'][j], p['b_lam_q2'][j], p['b_lam_k2'][j],
                            p['b_sub_norm'][j], p['b_w_o'][j], p['rel_bias'], lam_init, fresh)
            h = h + o
        fb = jnp.zeros((B, FFN_CONV - 1, 2 * D_FF), x.dtype) if fresh else st_ffn_conv[layer]
        o, fb_new = _conv_ffn(_rmsnorm(h, p['f_norm'][layer]), fb, p['f_w_up'][layer], p['f_w_conv'][layer],
                              p['f_b_conv'][layer], p['f_w_down'][layer])
        h = h + o
        new_fconv.append(fb_new)
        if layer == N_A_LAYERS - 1:
            kv = _rmsnorm(h, p['kv_norm']) @ p['w_kv']
            k_sh = kv[..., :DIFF_QK_W].reshape(B, L, 2 * DIFF_HEADS, DIFF_HEAD_DIM)
            v_sh = kv[..., DIFF_QK_W:].reshape(B, L, DIFF_HEADS, 2 * DIFF_HEAD_DIM)
    y = _rmsnorm(h, p['final_norm'])
    return y, jnp.stack(new_S), jnp.stack(new_gconv), jnp.stack(new_fconv), k_sh, v_sh


def setup_inputs(seed: int = 0) -> dict:
    key = jax.random.key(seed)
    ks = iter(list(jax.random.split(key, 48)))
    f32 = jnp.float32

    def nrm(shape, scale):
        return scale * jax.random.normal(next(ks), shape, f32)

    def gain(shape):
        return 1.0 + 0.01 * jax.random.normal(next(ks), shape, f32)

    dt = jnp.exp(jax.random.uniform(next(ks), (N_A_LAYERS, GDN_V_HEADS), f32,
                                    minval=math.log(1e-3), maxval=math.log(0.1)))
    return {
        'x_prompt': nrm((BATCH, SEQ, D_MODEL), 1.0),
        'x_sample': nrm((DEC_BATCH, DEC_SEQ, D_MODEL), 1.0),
        'state_gdn': nrm((N_A_LAYERS, DEC_BATCH, GDN_V_HEADS, GDN_HEAD_K, GDN_HEAD_V), 0.1),
        'state_gdn_conv': nrm((N_A_LAYERS, DEC_BATCH, GDN_CONV - 1, GDN_QKV_W), 1.0),
        'state_ffn_conv': nrm((DEPTH, DEC_BATCH, FFN_CONV - 1, 2 * D_FF), 1.0),
        'cache_k': nrm((DEC_BATCH, PAST_LEN, 2 * DIFF_HEADS, DIFF_HEAD_DIM), 1.0),
        'cache_v': nrm((DEC_BATCH, PAST_LEN, DIFF_HEADS, 2 * DIFF_HEAD_DIM), 1.0),
        'a_norm': gain((N_A_LAYERS, D_MODEL)),
        'a_w_in': nrm((N_A_LAYERS, D_MODEL, GDN_IN_W), D_MODEL ** -0.5),
        'a_w_conv': nrm((N_A_LAYERS, GDN_CONV, GDN_QKV_W), 0.5),
        'a_log': jnp.log(jax.random.uniform(next(ks), (N_A_LAYERS, GDN_V_HEADS), f32, minval=1.0, maxval=16.0)),
        'a_dt_bias': dt + jnp.log(-jnp.expm1(-dt)),
        'a_out_norm': gain((N_A_LAYERS, GDN_HEAD_V)),
        'a_w_out': nrm((N_A_LAYERS, GDN_V_W, D_MODEL), GDN_V_W ** -0.5),
        'kv_norm': gain((D_MODEL,)),
        'w_kv': nrm((D_MODEL, DIFF_QK_W + DIFF_V_W), D_MODEL ** -0.5),
        'b_norm': gain((N_B_LAYERS, D_MODEL)),
        'b_w_q': nrm((N_B_LAYERS, D_MODEL, DIFF_QK_W), D_MODEL ** -0.5),
        'b_lam_q1': nrm((N_B_LAYERS, DIFF_HEAD_DIM), 0.1),
        'b_lam_k1': nrm((N_B_LAYERS, DIFF_HEAD_DIM), 0.1),
        'b_lam_q2': nrm((N_B_LAYERS, DIFF_HEAD_DIM), 0.1),
        'b_lam_k2': nrm((N_B_LAYERS, DIFF_HEAD_DIM), 0.1),
        'b_sub_norm': gain((N_B_LAYERS, 2 * DIFF_HEAD_DIM)),
        'b_w_o': nrm((N_B_LAYERS, DIFF_V_W, D_MODEL), DIFF_V_W ** -0.5),
        'rel_bias': nrm((NUM_BUCKETS, DIFF_HEADS), 0.5),
        'f_norm': gain((DEPTH, D_MODEL)),
        'f_w_up': nrm((DEPTH, D_MODEL, 2 * D_FF), D_MODEL ** -0.5),
        'f_w_conv': nrm((DEPTH, FFN_CONV, 2 * D_FF), FFN_CONV ** -0.5),
        'f_b_conv': nrm((DEPTH, 2 * D_FF), 0.02),
        'f_w_down': nrm((DEPTH, D_FF, D_MODEL), D_FF ** -0.5),
        'final_norm': gain((D_MODEL,)),
    }


def reference(x_prompt, x_sample, state_gdn, state_gdn_conv, state_ffn_conv, cache_k, cache_v,
              a_norm, a_w_in, a_w_conv, a_log, a_dt_bias, a_out_norm, a_w_out,
              kv_norm, w_kv, b_norm, b_w_q, b_lam_q1, b_lam_k1, b_lam_q2, b_lam_k2, b_sub_norm, b_w_o,
              rel_bias, f_norm, f_w_up, f_w_conv, f_b_conv, f_w_down, final_norm):
    p = {
        'a_norm': a_norm, 'a_w_in': a_w_in, 'a_w_conv': a_w_conv, 'a_log': a_log,
        'a_dt_bias': a_dt_bias, 'a_out_norm': a_out_norm, 'a_w_out': a_w_out,
        'kv_norm': kv_norm, 'w_kv': w_kv,
        'b_norm': b_norm, 'b_w_q': b_w_q, 'b_lam_q1': b_lam_q1, 'b_lam_k1': b_lam_k1,
        'b_lam_q2': b_lam_q2, 'b_lam_k2': b_lam_k2, 'b_sub_norm': b_sub_norm, 'b_w_o': b_w_o,
        'rel_bias': rel_bias,
        'f_norm': f_norm, 'f_w_up': f_w_up, 'f_w_conv': f_w_conv, 'f_b_conv': f_b_conv,
        'f_w_down': f_w_down, 'final_norm': final_norm,
    }
    y_prompt, p_gdn, p_gdn_conv, p_ffn_conv, p_k, p_v = _trunk(
        x_prompt, 0, None, None, None, None, None, p)
    y_sample, s_gdn, s_gdn_conv, s_ffn_conv, s_k, s_v = _trunk(
        x_sample, cache_k.shape[1], state_gdn, state_gdn_conv, state_ffn_conv, cache_k, cache_v, p)
    return (y_prompt, y_sample, p_gdn, p_gdn_conv, p_ffn_conv, p_k, p_v,
            s_gdn, s_gdn_conv, s_ffn_conv, s_k, s_v)
```

```python
import functools
import math

import numpy as np
import jax
import jax.numpy as jnp
from jax import lax
from jax.experimental import pallas as pl
from jax.experimental.pallas import tpu as pltpu

F32 = jnp.float32
BF16 = jnp.bfloat16

CHUNK = 64
NORM_EPS = 1e-6
DIFF_SUBLN_EPS = 1e-5
L2_EPS = 1e-6
MAX_DISTANCE = 128
NEG_INF = -1e30

V7X_VMEM_LIMIT = 56 * 1024 * 1024
LANES = 128
F32_SUBLANES = 8
BF16_SUBLANES = 16


def _cparams(sem):
    return pltpu.CompilerParams(dimension_semantics=sem, vmem_limit_bytes=V7X_VMEM_LIMIT)


def _tile(n, pref, mult):
    if n <= pref:
        return n
    t = (pref // mult) * mult
    while t >= mult:
        if n % t == 0:
            return t
        t -= mult
    return n


def _dot(a, b):
    return jnp.dot(a, b, preferred_element_type=F32)


def _dot_nt(a, b):
    return lax.dot_general(a, b, (((1,), (1,)), ((), ())), preferred_element_type=F32)


def _dot_tn(a, b):
    return lax.dot_general(a, b, (((0,), (0,)), ((), ())), preferred_element_type=F32)


def _dot_hi(a, b):
    return jnp.dot(a, b, preferred_element_type=F32, precision=lax.Precision.HIGHEST)


def _silu(x):
    return x / (1.0 + jnp.exp(-x))


def _rms_body(x_ref, g_ref, o_ref):
    x = x_ref[...]
    ms = jnp.mean(x * x, axis=-1, keepdims=True)
    o_ref[...] = (x * lax.rsqrt(ms + NORM_EPS) * g_ref[...]).astype(o_ref.dtype)


def _rmsnorm_cast(x, g):
    T, D = x.shape
    tm = _tile(T, 512, BF16_SUBLANES)
    return pl.pallas_call(
        _rms_body,
        grid=(T // tm,),
        in_specs=[pl.BlockSpec((tm, D), lambda i: (i, 0)), pl.BlockSpec((1, D), lambda i: (0, 0))],
        out_specs=pl.BlockSpec((tm, D), lambda i: (i, 0)),
        out_shape=jax.ShapeDtypeStruct((T, D), BF16),
        compiler_params=_cparams(("parallel",)),
        name="rmsnorm_cast",
    )(x, g.reshape(1, D).astype(F32))


def _mm_body(x_ref, w_ref, *o_refs, epilogue):
    acc = _dot(x_ref[...], w_ref[...])
    for o_ref, val in zip(o_refs, epilogue(acc)):
        o_ref[...] = val.astype(o_ref.dtype)


def _matmul(x, w, epilogue, out_dtypes, name, col0=0, width=None, tm_pref=1024, tn_pref=1024):
    T, K = x.shape
    N = w.shape[1] if width is None else width
    tm = _tile(T, tm_pref, BF16_SUBLANES)
    tn = _tile(math.gcd(N, col0) if col0 else N, tn_pref, LANES)
    cb = col0 // tn
    outs = pl.pallas_call(
        functools.partial(_mm_body, epilogue=epilogue),
        grid=(N // tn, T // tm),
        in_specs=[pl.BlockSpec((tm, K), lambda n, m: (m, 0)), pl.BlockSpec((K, tn), lambda n, m: (0, n + cb))],
        out_specs=[pl.BlockSpec((tm, tn), lambda n, m: (m, n)) for _ in out_dtypes],
        out_shape=[jax.ShapeDtypeStruct((T, N), dt) for dt in out_dtypes],
        compiler_params=_cparams(("parallel", "parallel")),
        name=name,
    )(x, w)
    return outs


def _gate_body(x_ref, wb_ref, wa_ref, alog_ref, dtb_ref, beta_ref, g_ref):
    x = x_ref[...]
    b = _dot(x, wb_ref[...])
    a = _dot(x, wa_ref[...]) + dtb_ref[...]
    beta_ref[...] = 1.0 / (1.0 + jnp.exp(-b))
    softplus = jnp.maximum(a, 0.0) + jnp.log(1.0 + jnp.exp(-jnp.abs(a)))
    g_ref[...] = -jnp.exp(alog_ref[...]) * softplus


def _gate_proj(xn, w_b, w_a, a_log, dt_bias):
    T, K = xn.shape
    H = w_b.shape[1]
    pad = LANES - H
    wb = jnp.pad(w_b, ((0, 0), (0, pad))).astype(BF16)
    wa = jnp.pad(w_a, ((0, 0), (0, pad))).astype(BF16)
    al = jnp.pad(a_log.astype(F32), (0, pad)).reshape(1, LANES)
    db = jnp.pad(dt_bias.astype(F32), (0, pad)).reshape(1, LANES)
    tm = _tile(T, 1024, BF16_SUBLANES)
    row = pl.BlockSpec((tm, K), lambda i: (i, 0))
    wsp = pl.BlockSpec((K, LANES), lambda i: (0, 0))
    vsp = pl.BlockSpec((1, LANES), lambda i: (0, 0))
    osp = pl.BlockSpec((tm, LANES), lambda i: (i, 0))
    beta, g = pl.pallas_call(
        _gate_body,
        grid=(T // tm,),
        in_specs=[row, wsp, wsp, vsp, vsp],
        out_specs=[osp, osp],
        out_shape=[jax.ShapeDtypeStruct((T, LANES), F32)] * 2,
        compiler_params=_cparams(("parallel",)),
        name="gdn_gate_proj",
    )(xn, wb, wa, al, db)
    return beta[:, :H], g[:, :H]


def _convmm_body(*refs, G, W, stride, tm, pad, tps, glu):
    hist = (W - 1) * stride
    it = iter(refs)
    x_ref = next(it)
    w_refs = [next(it) for _ in range(G)]
    wc_refs = [next(it) for _ in range(G)]
    b_refs = [next(it) for _ in range(G)] if glu else None
    st_refs = [next(it) for _ in range(G)]
    out_ref = next(it)
    nst_refs = [next(it) for _ in range(G)]
    yscs = [next(it) for _ in range(G)]

    m = pl.program_id(1)
    first = (m % tps) == 0
    last = (m % tps) == tps - 1
    x = x_ref[...]
    convs = []
    for g in range(G):
        ysc = yscs[g]

        @pl.when(first)
        def _():
            ysc[pad - hist:pad, :] = st_refs[g][...]

        @pl.when(jnp.logical_not(first))
        def _():
            ysc[0:pad, :] = ysc[tm:tm + pad, :]

        ysc[pad:pad + tm, :] = _dot(x, w_refs[g][...])

        @pl.when(last)
        def _():
            nst_refs[g][...] = ysc[pad + tm - hist:pad + tm, :]

        acc = None
        for i in range(W):
            off = pad - (W - 1 - i) * stride
            term = ysc[off:off + tm, :] * wc_refs[g][i:i + 1, :]
            acc = term if acc is None else acc + term
        if glu:
            acc = acc + b_refs[g][...]
        convs.append(acc)
    if glu:
        res = _silu(convs[0]) * convs[1]
    else:
        res = _silu(convs[0])
    out_ref[...] = res.astype(out_ref.dtype)


def _conv_matmul(x, w, wc, bias, state, *, groups, width, stride, nseq, glu, name, tm_pref, tn_pref):
    T, K = x.shape
    W = wc.shape[0]
    Ng = width // groups
    hist = (W - 1) * stride
    rows = T // nseq
    tm = _tile(rows, tm_pref, BF16_SUBLANES)
    tn = _tile(Ng, tn_pref, LANES)
    tps = rows // tm
    pad = -(-hist // F32_SUBLANES) * F32_SUBLANES
    assert tm >= pad and state.shape == (nseq, hist, groups * Ng)
    nb = Ng // tn

    def col(g):
        return lambda n, m: (0, n + g * nb)

    def stcol(g):
        return lambda n, m: (m // tps, 0, n + g * nb)

    in_specs = [pl.BlockSpec((tm, K), lambda n, m: (m, 0))]
    args = [x]
    in_specs += [pl.BlockSpec((K, tn), col(g)) for g in range(groups)]
    args += [w] * groups
    in_specs += [pl.BlockSpec((W, tn), col(g)) for g in range(groups)]
    args += [wc.astype(F32)] * groups
    if glu:
        in_specs += [pl.BlockSpec((1, tn), col(g)) for g in range(groups)]
        args += [bias.astype(F32).reshape(1, -1)] * groups
    in_specs += [pl.BlockSpec((None, hist, tn), stcol(g)) for g in range(groups)]
    args += [state.astype(F32)] * groups
    out_specs = [pl.BlockSpec((tm, tn), lambda n, m: (m, n))]
    out_specs += [pl.BlockSpec((None, hist, tn), stcol(0)) for g in range(groups)]
    out_shape = [jax.ShapeDtypeStruct((T, Ng), BF16)]
    out_shape += [jax.ShapeDtypeStruct((nseq, hist, Ng), F32) for g in range(groups)]
    outs = pl.pallas_call(
        functools.partial(_convmm_body, G=groups, W=W, stride=stride, tm=tm, pad=pad, tps=tps, glu=glu),
        grid=(nb, T // tm),
        in_specs=in_specs,
        out_specs=out_specs,
        out_shape=out_shape,
        scratch_shapes=[pltpu.VMEM((pad + tm, tn), F32) for _ in range(groups)],
        compiler_params=_cparams(("parallel", "arbitrary")),
        name=name,
    )(*args)
    new_state = outs[1] if groups == 1 else jnp.concatenate(outs[1:], axis=-1)
    return outs[0], new_state


def _gdn_body(q_ref, k_ref, v_ref, z_ref, beta_ref, g_ref, s0_ref, gn_ref, o_ref, sout_ref, S,
              *, hb, rep, dk, dv, n_chunks):
    n = pl.program_id(2)

    @pl.when(n == 0)
    def _():
        S[...] = s0_ref[0]

    C = q_ref.shape[0]
    ri = lax.broadcasted_iota(jnp.int32, (C, C), 0)
    ci = lax.broadcasted_iota(jnp.int32, (C, C), 1)
    incl = ri >= ci
    strict = ri > ci
    eye = (ri == ci).astype(F32)
    gall = g_ref[...]
    Gc = _dot_hi(incl.astype(F32), gall)
    beta = beta_ref[...]
    gain = gn_ref[...]

    n_sq = int(round(math.log2(C))) - 1
    for jk in range(hb // rep):
        kf = k_ref[:, jk * dk:(jk + 1) * dk].astype(F32)
        qf = q_ref[:, jk * dk:(jk + 1) * dk].astype(F32)
        kn = kf * lax.rsqrt(jnp.sum(kf * kf, axis=-1, keepdims=True) + L2_EPS)
        qn = qf * lax.rsqrt(jnp.sum(qf * qf, axis=-1, keepdims=True) + L2_EPS) * (dk ** -0.5)
        kb = kn.astype(BF16)
        kk = _dot_nt(kb, kb)
        qk = _dot_nt(qn.astype(BF16), kb)
        for r in range(rep):
            j = jk * rep + r
            Gcol = Gc[:, j:j + 1]
            Grow = jnp.sum(eye * Gcol, axis=0, keepdims=True)
            diff = Gcol - Grow
            dec_incl = jnp.where(incl, jnp.exp(jnp.where(incl, diff, 0.0)), 0.0)
            dec_strict = jnp.where(strict, dec_incl, 0.0)
            bcol = beta[:, j:j + 1]
            Nm = -(bcol * kk * dec_strict)
            Tm = eye + Nm
            Np = Nm
            for _ in range(n_sq):
                Np = _dot_hi(Np, Np)
                Tm = Tm + _dot_hi(Tm, Np)
            eG = jnp.exp(Gcol)
            vf = v_ref[:, j * dv:(j + 1) * dv].astype(F32)
            rhs = jnp.concatenate([bcol * vf, (bcol * eG) * kn], axis=1)
            sol = _dot_hi(Tm, rhs)
            u_v = sol[:, :dv]
            w = sol[:, dv:]
            P = qk * dec_incl
            q_g = qn * eG
            Glast = Gc[C - 1:C, j:j + 1]
            k_d = kn * jnp.exp(Glast - Gcol)
            Sj = S[j]
            Sb = Sj.astype(BF16)
            u = u_v - _dot(w.astype(BF16), Sb)
            ub = u.astype(BF16)
            o = _dot(q_g.astype(BF16), Sb) + _dot(P.astype(BF16), ub)
            S[j] = jnp.exp(Glast) * Sj + _dot_tn(k_d.astype(BF16), ub)
            on = o * lax.rsqrt(jnp.mean(o * o, axis=-1, keepdims=True) + NORM_EPS) * gain
            o_ref[:, j * dv:(j + 1) * dv] = (on * z_ref[:, j * dv:(j + 1) * dv].astype(F32)).astype(o_ref.dtype)

    @pl.when(n == n_chunks - 1)
    def _():
        sout_ref[0] = S[...]


def _gdn_core(qkv, sz, beta, g, S0, out_norm, *, nseq, chunk, hb):
    T = qkv.shape[0]
    _, HV, dk, dv = S0.shape
    VW = HV * dv
    QK = (qkv.shape[1] - VW) // 2
    HK = QK // dk
    rep = HV // HK
    HG = HV // hb
    kb = hb // rep
    L = T // nseq
    n_chunks = L // chunk

    def group_major(t):
        t = t.reshape(T, HG, hb).transpose(1, 0, 2)
        return jnp.pad(t, ((0, 0), (0, 0), (0, LANES - hb)))

    rowblk = lambda b, h, n: b * n_chunks + n
    in_specs = [
        pl.BlockSpec((chunk, kb * dk), lambda b, h, n: (rowblk(b, h, n), h)),
        pl.BlockSpec((chunk, kb * dk), lambda b, h, n: (rowblk(b, h, n), HG + h)),
        pl.BlockSpec((chunk, hb * dv), lambda b, h, n: (rowblk(b, h, n), 2 * QK // (hb * dv) + h)),
        pl.BlockSpec((chunk, hb * dv), lambda b, h, n: (rowblk(b, h, n), h)),
        pl.BlockSpec((None, chunk, LANES), lambda b, h, n: (h, rowblk(b, h, n), 0)),
        pl.BlockSpec((None, chunk, LANES), lambda b, h, n: (h, rowblk(b, h, n), 0)),
        pl.BlockSpec((1, hb, dk, dv), lambda b, h, n: (b, h, 0, 0)),
        pl.BlockSpec((1, dv), lambda b, h, n: (0, 0)),
    ]
    out_specs = [
        pl.BlockSpec((chunk, hb * dv), lambda b, h, n: (rowblk(b, h, n), h)),
        pl.BlockSpec((1, hb, dk, dv), lambda b, h, n: (b, h, 0, 0)),
    ]
    o, S = pl.pallas_call(
        functools.partial(_gdn_body, hb=hb, rep=rep, dk=dk, dv=dv, n_chunks=n_chunks),
        grid=(nseq, HG, n_chunks),
        in_specs=in_specs,
        out_specs=out_specs,
        out_shape=[jax.ShapeDtypeStruct((T, VW), BF16), jax.ShapeDtypeStruct(S0.shape, F32)],
        scratch_shapes=[pltpu.VMEM((hb, dk, dv), F32)],
        compiler_params=_cparams(("parallel", "parallel", "arbitrary")),
        name="gdn_core",
    )(qkv, qkv, qkv, sz, group_major(beta), group_major(g), S0.astype(F32),
      out_norm.astype(F32).reshape(1, dv))
    return o, S


def _mmres_body(x_ref, w_ref, res_ref, *refs, n_norm, with_h):
    gains = refs[:n_norm]
    outs = refs[n_norm:]
    h = res_ref[...] + _dot(x_ref[...], w_ref[...])
    k = 0
    if with_h:
        outs[0][...] = h
        k = 1
    if n_norm:
        inv = lax.rsqrt(jnp.mean(h * h, axis=-1, keepdims=True) + NORM_EPS)
        y = h * inv
        for i in range(n_norm):
            outs[k + i][...] = (y * gains[i][...]).astype(outs[k + i].dtype)


def _matmul_residual(x, w, res, gains, norm_dtype, with_h, name):
    T, K = x.shape
    D = w.shape[1]
    tm = _tile(T, 256, BF16_SUBLANES)
    row = lambda i: (i, 0)
    fixed = lambda i: (0, 0)
    n_norm = len(gains)
    in_specs = [pl.BlockSpec((tm, K), row),
                pl.BlockSpec((K, D), fixed, pipeline_mode=pl.Buffered(1)),
                pl.BlockSpec((tm, D), row)]
    in_specs += [pl.BlockSpec((1, D), fixed) for _ in gains]
    out_shape = ([jax.ShapeDtypeStruct((T, D), F32)] if with_h else []) \
        + [jax.ShapeDtypeStruct((T, D), norm_dtype) for _ in gains]
    outs = pl.pallas_call(
        functools.partial(_mmres_body, n_norm=n_norm, with_h=with_h),
        grid=(T // tm,),
        in_specs=in_specs,
        out_specs=[pl.BlockSpec((tm, D), row) for _ in out_shape],
        out_shape=out_shape,
        compiler_params=_cparams(("parallel",)),
        name=name,
    )(x, w, res, *[g.astype(F32).reshape(1, D) for g in gains])
    return outs


def _bucket_table(num_buckets):
    half = num_buckets // 2
    exact = half // 2
    n = np.arange(MAX_DISTANCE + 1)
    large = exact + (np.log(np.maximum(n, 1).astype(np.float32) / np.float32(exact))
                     / np.float32(math.log(MAX_DISTANCE / exact)) * np.float32(half - exact)).astype(np.int32)
    large = np.minimum(large, half - 1)
    return np.where(n < exact, n, large), half


def _bias_body(rb_ref, out_ref, *, tiles, steps, half):
    h = pl.program_id(0)
    for t, (q0, k0) in enumerate(tiles):
        nq, nk = out_ref.shape[2], out_ref.shape[3]
        qpos = q0 + lax.broadcasted_iota(jnp.int32, (nq, nk), 0)
        kpos = k0 + lax.broadcasted_iota(jnp.int32, (nq, nk), 1)
        rel = kpos - qpos
        n = jnp.abs(rel)
        neg = jnp.full((nq, nk), rb_ref[0, h], F32)
        pos = jnp.full((nq, nk), rb_ref[half, h], F32)
        for thr, b in steps:
            ge = n >= thr
            neg = jnp.where(ge, rb_ref[b, h], neg)
            pos = jnp.where(ge, rb_ref[half + b, h], pos)
        bias = jnp.where(rel > 0, pos, neg)
        visible = kpos // CHUNK <= qpos // CHUNK
        out_ref[t, 0] = jnp.where(visible, bias, NEG_INF)


def _bias_tiles(rel_bias, tiles, nq, nk):
    NB, H = rel_bias.shape
    table, half = _bucket_table(NB)
    steps = [(int(i), int(table[i])) for i in range(1, len(table)) if table[i] != table[i - 1]]
    return pl.pallas_call(
        functools.partial(_bias_body, tiles=tuple(tiles), steps=tuple(steps), half=half),
        grid=(H,),
        in_specs=[pl.BlockSpec(memory_space=pltpu.SMEM)],
        out_specs=pl.BlockSpec((len(tiles), 1, nq, nk), lambda h: (0, h, 0, 0)),
        out_shape=jax.ShapeDtypeStruct((len(tiles), H, nq, nk), F32),
        compiler_params=_cparams(("parallel",)),
        name="t5_bias_tiles",
    )(rel_bias.astype(F32))


def _softmax_update(s, vb, m_ref, l_ref, acc_ref, idx):
    m_prev = m_ref[idx]
    l_prev = l_ref[idx]
    m_new = jnp.maximum(m_prev, jnp.max(s, axis=1, keepdims=True))
    alpha = jnp.exp(m_prev - m_new)
    p = jnp.exp(s - m_new[:, :1])
    l_ref[idx] = alpha * l_prev + jnp.sum(p, axis=1, keepdims=True)
    acc_ref[idx] = acc_ref[idx] * alpha[:, :1] + _dot(p.astype(BF16), vb)
    m_ref[idx] = m_new


def _diff_finalize(lam_ref, sn_ref, o_ref, l_ref, acc_ref, *, H, d, lam_init):
    lv = lam_ref[...]
    lam = (jnp.exp(jnp.sum(lv[0:1] * lv[1:2], axis=1, keepdims=True))
           - jnp.exp(jnp.sum(lv[2:3] * lv[3:4], axis=1, keepdims=True)) + lam_init)
    for h in range(H):
        o = acc_ref[2 * h] / l_ref[2 * h][:, :1] - lam * (acc_ref[2 * h + 1] / l_ref[2 * h + 1][:, :1])
        on = o * lax.rsqrt(jnp.mean(o * o, axis=-1, keepdims=True) + DIFF_SUBLN_EPS) * sn_ref[...]
        o_ref[:, h * 2 * d:(h + 1) * 2 * d] = (on * (1.0 - lam_init)).astype(o_ref.dtype)


def _attn_body(qi_ref, kj_ref, q_ref, k_ref, v_ref, bias_ref, lam_ref, sn_ref, o_ref, m_ref, l_ref, acc_ref,
               *, H, d, lam_init):
    p = pl.program_id(1)
    qi = qi_ref[p]
    kj = kj_ref[p]

    @pl.when(kj == 0)
    def _():
        m_ref[...] = jnp.full(m_ref.shape, -jnp.inf, F32)
        l_ref[...] = jnp.zeros(l_ref.shape, F32)
        acc_ref[...] = jnp.zeros(acc_ref.shape, F32)

    bidx = jnp.minimum(qi - kj, 2)
    for h in range(H):
        vb = v_ref[:, h * 2 * d:(h + 1) * 2 * d]
        for mp in range(2):
            c0 = (2 * h + mp) * d
            s = _dot_nt(q_ref[:, c0:c0 + d], k_ref[:, c0:c0 + d]) + bias_ref[bidx, h]
            _softmax_update(s, vb, m_ref, l_ref, acc_ref, 2 * h + mp)

    @pl.when(kj == qi)
    def _():
        _diff_finalize(lam_ref, sn_ref, o_ref, l_ref, acc_ref, H=H, d=d, lam_init=lam_init)


def _diff_attention_prompt(q, k, v, rel_bias, lam_vecs, sub_norm, lam_init, *, nseq, d):
    T, QW = q.shape
    H = QW // (2 * d)
    L = T // nseq
    tq = _tile(L, 256, LANES)
    assert tq % CHUNK == 0 and tq >= MAX_DISTANCE and L % tq == 0
    nq = L // tq
    pairs = [(i, j) for i in range(nq) for j in range(i + 1)]
    qi = jnp.asarray([pq for pq, _ in pairs], jnp.int32)
    kj = jnp.asarray([pk for _, pk in pairs], jnp.int32)
    bias = _bias_tiles(rel_bias, [(0, 0), (tq, 0), (2 * tq, 0)], tq, tq)
    qrow = lambda b, p, qi_r, kj_r: (b * nq + qi_r[p], 0)
    krow = lambda b, p, qi_r, kj_r: (b * nq + kj_r[p], 0)
    fixed2 = lambda b, p, qi_r, kj_r: (0, 0)
    grid_spec = pltpu.PrefetchScalarGridSpec(
        num_scalar_prefetch=2,
        grid=(nseq, len(pairs)),
        in_specs=[
            pl.BlockSpec((tq, QW), qrow),
            pl.BlockSpec((tq, QW), krow),
            pl.BlockSpec((tq, QW), krow),
            pl.BlockSpec((3, H, tq, tq), lambda b, p, qi_r, kj_r: (0, 0, 0, 0), pipeline_mode=pl.Buffered(1)),
            pl.BlockSpec((4, d), fixed2),
            pl.BlockSpec((1, 2 * d), fixed2),
        ],
        out_specs=pl.BlockSpec((tq, QW), qrow),
        scratch_shapes=[pltpu.VMEM((2 * H, tq, LANES), F32), pltpu.VMEM((2 * H, tq, LANES), F32),
                        pltpu.VMEM((2 * H, tq, 2 * d), F32)],
    )
    return pl.pallas_call(
        functools.partial(_attn_body, H=H, d=d, lam_init=lam_init),
        grid_spec=grid_spec,
        out_shape=jax.ShapeDtypeStruct((T, QW), BF16),
        compiler_params=_cparams(("parallel", "arbitrary")),
        name="diff_attention_prompt",
    )(qi, kj, q, k, v, bias, lam_vecs, sub_norm.astype(F32).reshape(1, 2 * d))


def _attn_dec_body(q_ref, ck_ref, cv_ref, kn_ref, vn_ref, bc_ref, bn_ref, lam_ref, sn_ref, o_ref,
                   m_ref, l_ref, acc_ref, *, H, d, lam_init, n_blocks):
    kb_i = pl.program_id(1)

    @pl.when(kb_i == 0)
    def _():
        m_ref[...] = jnp.full(m_ref.shape, -jnp.inf, F32)
        l_ref[...] = jnp.zeros(l_ref.shape, F32)
        acc_ref[...] = jnp.zeros(acc_ref.shape, F32)

    def sweep(k_all, v_all, b_ref):
        for h in range(H):
            vb = v_all[:, h * 2 * d:(h + 1) * 2 * d]
            for mp in range(2):
                c0 = (2 * h + mp) * d
                s = _dot_nt(q_ref[:, c0:c0 + d], k_all[:, c0:c0 + d]) + b_ref[h]
                _softmax_update(s, vb, m_ref, l_ref, acc_ref, 2 * h + mp)

    sweep(ck_ref[0].astype(BF16), cv_ref[0].astype(BF16), bc_ref)

    @pl.when(kb_i == n_blocks - 1)
    def _():
        sweep(kn_ref[...], vn_ref[...], bn_ref)
        _diff_finalize(lam_ref, sn_ref, o_ref, l_ref, acc_ref, H=H, d=d, lam_init=lam_init)


def _diff_attention_decode(q, k_new, v_new, cache_k, cache_v, rel_bias, lam_vecs, sub_norm, lam_init, *, d):
    B, P, QW = cache_k.shape
    H = QW // (2 * d)
    Lq = q.shape[0] // B
    tk = _tile(P, 512, LANES)
    nb = P // tk
    bias_c = _bias_tiles(rel_bias, [(P, j * tk) for j in range(nb)], Lq, tk)
    bias_n = _bias_tiles(rel_bias, [(P, P)], Lq, Lq)[0]
    seq = lambda b, j: (b, 0)
    fixed2 = lambda b, j: (0, 0)
    return pl.pallas_call(
        functools.partial(_attn_dec_body, H=H, d=d, lam_init=lam_init, n_blocks=nb),
        grid=(B, nb),
        in_specs=[
            pl.BlockSpec((Lq, QW), seq),
            pl.BlockSpec((1, tk, QW), lambda b, j: (b, j, 0)),
            pl.BlockSpec((1, tk, QW), lambda b, j: (b, j, 0)),
            pl.BlockSpec((Lq, QW), seq),
            pl.BlockSpec((Lq, QW), seq),
            pl.BlockSpec((None, H, Lq, tk), lambda b, j: (j, 0, 0, 0)),
            pl.BlockSpec((H, Lq, Lq), lambda b, j: (0, 0, 0)),
            pl.BlockSpec((4, d), fixed2),
            pl.BlockSpec((1, 2 * d), fixed2),
        ],
        out_specs=pl.BlockSpec((Lq, QW), seq),
        out_shape=jax.ShapeDtypeStruct((B * Lq, QW), BF16),
        scratch_shapes=[pltpu.VMEM((2 * H, Lq, LANES), F32), pltpu.VMEM((2 * H, Lq, LANES), F32),
                        pltpu.VMEM((2 * H, Lq, 2 * d), F32)],
        compiler_params=_cparams(("parallel", "arbitrary")),
        name="diff_attention_decode",
    )(q, cache_k, cache_v, k_new, v_new, bias_c, bias_n, lam_vecs, sub_norm.astype(F32).reshape(1, 2 * d))


def _trunk(x, p, st_gdn, st_gdn_conv, st_ffn_conv, cache_k, cache_v):
    B, L, D = x.shape
    fresh = cache_k is None
    T = B * L
    depth = p['f_norm'].shape[0]
    n_a = p['a_norm'].shape[0]
    HV, dv = p['a_log'].shape[1], p['a_out_norm'].shape[1]
    VW = HV * dv
    QKVW = p['a_w_conv'].shape[2]
    dk = p['gdn_head_k']
    d = p['b_lam_q1'].shape[1]
    DFF = p['f_w_down'].shape[1]
    QW = p['b_w_q'].shape[2]

    time_major = not fresh
    if time_major:
        h = x.transpose(1, 0, 2).reshape(T, D)
        stride, nseq_conv = B, 1
    else:
        h = x.reshape(T, D)
        stride, nseq_conv = 1, B

    def to_seq(t):
        return t.reshape(L, B, -1).transpose(1, 0, 2).reshape(T, -1) if time_major else t

    def to_time(t):
        return t.reshape(B, L, -1).transpose(1, 0, 2).reshape(T, -1) if time_major else t

    def conv_state_in(st, width, chans):
        if st is None:
            return jnp.zeros((nseq_conv, (width - 1) * stride, chans), F32)
        return st.transpose(1, 0, 2).reshape(1, (width - 1) * B, chans)

    def conv_state_out(st, width):
        if time_major:
            return st.reshape(width - 1, B, -1).transpose(1, 0, 2)
        return st

    new_S, new_gconv, new_fconv = [], [], []
    k_f32 = v_f32 = k_bf = v_bf = None
    xn = _rmsnorm_cast(h, p['a_norm'][0]) if n_a > 0 else _rmsnorm_cast(h, p['b_norm'][0])
    xn_kv = None
    y = None
    for layer in range(depth):
        if layer < n_a:
            i = layer
            w_in = p['a_w_in_bf'][i]
            w_ba = p['a_w_in'][i][:, QKVW + VW:]
            gw = p['a_w_conv'].shape[1]
            cst = conv_state_in(None if fresh else st_gdn_conv[i], gw, QKVW)
            qkv, gconv = _conv_matmul(xn, w_in, p['a_w_conv'][i], None, cst, groups=1, width=QKVW,
                                      stride=stride, nseq=nseq_conv, glu=False, name="gdn_qkv_conv",
                                      tm_pref=512, tn_pref=1024)
            (sz,) = _matmul(xn, w_in, lambda acc: (_silu(acc),), [BF16], "gdn_z_proj", col0=QKVW, width=VW)
            beta, g = _gate_proj(xn, w_ba[:, :HV], w_ba[:, HV:], p['a_log'][i], p['a_dt_bias'][i])
            S0 = jnp.zeros((B, HV, dk, dv), F32) if fresh else st_gdn[i]
            og, S = _gdn_core(to_seq(qkv), to_seq(sz), to_seq(beta), to_seq(g), S0, p['a_out_norm'][i],
                              nseq=B, chunk=min(CHUNK, L), hb=min(8, HV))
            new_S.append(S)
            new_gconv.append(conv_state_out(gconv, gw))
            h, xn = _matmul_residual(to_time(og), p['a_w_out_bf'][i], h, [p['f_norm'][layer]],
                                     BF16, True, "gdn_out_proj")
        else:
            j = layer - n_a
            lam_init = 0.8 - 0.6 * math.exp(-0.3 * layer)
            scale = d ** -0.5
            (q,) = _matmul(xn, p['b_w_q_bf'][j], lambda acc: (acc * scale,), [BF16], "diff_q_proj")
            lam_vecs = jnp.stack([p['b_lam_q1'][j], p['b_lam_k1'][j], p['b_lam_q2'][j],
                                  p['b_lam_k2'][j]]).astype(F32)
            if fresh:
                ao = _diff_attention_prompt(q, k_bf, v_bf, p['rel_bias'], lam_vecs, p['b_sub_norm'][j],
                                            lam_init, nseq=B, d=d)
            else:
                ao = _diff_attention_decode(to_seq(q), to_seq(k_bf), to_seq(v_bf),
                                            cache_k.reshape(B, -1, QW), cache_v.reshape(B, -1, QW),
                                            p['rel_bias'], lam_vecs, p['b_sub_norm'][j], lam_init, d=d)
                ao = to_time(ao)
            h, xn = _matmul_residual(ao, p['b_w_o_bf'][j], h, [p['f_norm'][layer]],
                                     BF16, True, "diff_out_proj")
        fw = p['f_w_conv'].shape[1]
        fst = conv_state_in(None if fresh else st_ffn_conv[layer], fw, 2 * DFF)
        act, fconv = _conv_matmul(xn, p['f_w_up_bf'][layer], p['f_w_conv'][layer],
                                  p['f_b_conv'][layer], fst, groups=2, width=2 * DFF, stride=stride,
                                  nseq=nseq_conv, glu=True, name="ffn_up_conv", tm_pref=512, tn_pref=512)
        new_fconv.append(conv_state_out(fconv, fw))
        last = layer == depth - 1
        gains = []
        if layer == n_a - 1:
            gains.append(p['kv_norm'])
        if last:
            gains.append(p['final_norm'])
        elif layer + 1 < n_a:
            gains.append(p['a_norm'][layer + 1])
        else:
            gains.append(p['b_norm'][layer + 1 - n_a])
        outs = _matmul_residual(act, p['f_w_down_bf'][layer], h, gains,
                                F32 if last else BF16, not last, "ffn_down_proj")
        if last:
            if layer == n_a - 1:
                xn_kv = outs[0].astype(BF16)
            y = outs[-1]
        else:
            h = outs[0]
            if layer == n_a - 1:
                xn_kv = outs[1]
            xn = outs[-1]
        if layer == n_a - 1:
            both = lambda acc: (acc, acc)
            k_f32, k_bf = _matmul(xn_kv, p['w_kv_bf'], both, [F32, BF16], "k_proj", col0=0, width=QW)
            v_f32, v_bf = _matmul(xn_kv, p['w_kv_bf'], both, [F32, BF16], "v_proj", col0=QW, width=QW)

    n_kh = 2 * (QW // (2 * d))
    y = to_seq(y).reshape(B, L, D)
    k_sh = to_seq(k_f32).reshape(B, L, n_kh, d)
    v_sh = to_seq(v_f32).reshape(B, L, n_kh // 2, 2 * d)
    return y, jnp.stack(new_S), jnp.stack(new_gconv), jnp.stack(new_fconv), k_sh, v_sh


def kernel(x_prompt, x_sample, state_gdn, state_gdn_conv, state_ffn_conv, cache_k, cache_v, a_norm, a_w_in, a_w_conv, a_log, a_dt_bias, a_out_norm, a_w_out, kv_norm, w_kv, b_norm, b_w_q, b_lam_q1, b_lam_k1, b_lam_q2, b_lam_k2, b_sub_norm, b_w_o, rel_bias, f_norm, f_w_up, f_w_conv, f_b_conv, f_w_down, final_norm):
    p = {
        'a_norm': a_norm, 'a_w_in': a_w_in, 'a_w_conv': a_w_conv, 'a_log': a_log,
        'a_dt_bias': a_dt_bias, 'a_out_norm': a_out_norm, 'a_w_out': a_w_out,
        'kv_norm': kv_norm, 'w_kv': w_kv,
        'b_norm': b_norm, 'b_w_q': b_w_q, 'b_lam_q1': b_lam_q1, 'b_lam_k1': b_lam_k1,
        'b_lam_q2': b_lam_q2, 'b_lam_k2': b_lam_k2, 'b_sub_norm': b_sub_norm, 'b_w_o': b_w_o,
        'rel_bias': rel_bias,
        'f_norm': f_norm, 'f_w_up': f_w_up, 'f_w_conv': f_w_conv, 'f_b_conv': f_b_conv,
        'f_w_down': f_w_down, 'final_norm': final_norm,
        'gdn_head_k': state_gdn.shape[3],
    }
    for name in ('a_w_in', 'a_w_out', 'w_kv', 'b_w_q', 'b_w_o', 'f_w_up', 'f_w_down'):
        p[name + '_bf'] = p[name].astype(BF16)
    out_p = _trunk(x_prompt, p, None, None, None, None, None)
    out_s = _trunk(x_sample, p, state_gdn, state_gdn_conv, state_ffn_conv, cache_k, cache_v)
    return (out_p[0], out_s[0]) + tuple(out_p[1:]) + tuple(out_s[1:])
```

```python
import functools
import math

import numpy as np
import jax
import jax.numpy as jnp
from jax import lax
from jax.experimental import pallas as pl
from jax.experimental.pallas import tpu as pltpu

F32 = jnp.float32
BF16 = jnp.bfloat16

CHUNK = 64
NORM_EPS = 1e-6
DIFF_SUBLN_EPS = 1e-5
L2_EPS = 1e-6
MAX_DISTANCE = 128
NEG_INF = -1e30
CONV_ROW_CHUNK = 512

V7X_VMEM_LIMIT = 56 * 1024 * 1024
LANES = 128
V7X_MXU_COLS = 256
F32_SUBLANES = 8
BF16_SUBLANES = 16


def _cparams(sem):
    return pltpu.CompilerParams(dimension_semantics=sem, vmem_limit_bytes=V7X_VMEM_LIMIT)


def _tile(n, pref, mult):
    if n <= pref:
        return n
    t = (pref // mult) * mult
    while t >= mult:
        if n % t == 0:
            return t
        t -= mult
    return n


def _dot(a, b):
    return jnp.dot(a, b, preferred_element_type=F32)


def _dot_nt(a, b):
    return lax.dot_general(a, b, (((1,), (1,)), ((), ())), preferred_element_type=F32)


def _dot_tn(a, b):
    return lax.dot_general(a, b, (((0,), (0,)), ((), ())), preferred_element_type=F32)


def _dot_hi(a, b):
    return jnp.dot(a, b, preferred_element_type=F32, precision=lax.Precision.HIGHEST)


def _split_bf16(x):
    hi = x.astype(BF16).astype(F32)
    return hi, x - hi


def _dot_x4(x, y):
    xh, xl = _split_bf16(x)
    yh, yl = _split_bf16(y)
    xx = jnp.concatenate([xh, xl], axis=1).astype(BF16)
    lhs = jnp.concatenate([xx, xx], axis=1)
    rhs = jnp.concatenate([yh, yh, yl, yl], axis=0).astype(BF16)
    return _dot(lhs, rhs)


def _silu(x):
    return x / (1.0 + jnp.exp(-x))


def _rms_body(x_ref, g_ref, o_ref):
    x = x_ref[...]
    ms = jnp.mean(x * x, axis=-1, keepdims=True)
    o_ref[...] = (x * lax.rsqrt(ms + NORM_EPS) * g_ref[...]).astype(o_ref.dtype)


def _rmsnorm_cast(x, g):
    T, D = x.shape
    tm = _tile(T, 512, BF16_SUBLANES)
    return pl.pallas_call(
        _rms_body,
        grid=(T // tm,),
        in_specs=[pl.BlockSpec((tm, D), lambda i: (i, 0)), pl.BlockSpec((1, D), lambda i: (0, 0))],
        out_specs=pl.BlockSpec((tm, D), lambda i: (i, 0)),
        out_shape=jax.ShapeDtypeStruct((T, D), BF16),
        compiler_params=_cparams(("parallel",)),
        name="rmsnorm_cast",
    )(x, g.reshape(1, D).astype(F32))


def _mm_body(x_ref, w_ref, *o_refs, epilogue):
    acc = _dot(x_ref[...], w_ref[...])
    for o_ref, val in zip(o_refs, epilogue(acc)):
        o_ref[...] = val.astype(o_ref.dtype)


def _matmul(x, w, epilogue, out_dtypes, name, col0=0, width=None, tm_pref=1024, tn_pref=1024):
    T, K = x.shape
    N = w.shape[1] if width is None else width
    tm = _tile(T, tm_pref, BF16_SUBLANES)
    tn = _tile(math.gcd(N, col0) if col0 else N, tn_pref, LANES)
    cb = col0 // tn
    outs = pl.pallas_call(
        functools.partial(_mm_body, epilogue=epilogue),
        grid=(N // tn, T // tm),
        in_specs=[pl.BlockSpec((tm, K), lambda n, m: (m, 0)), pl.BlockSpec((K, tn), lambda n, m: (0, n + cb))],
        out_specs=[pl.BlockSpec((tm, tn), lambda n, m: (m, n)) for _ in out_dtypes],
        out_shape=[jax.ShapeDtypeStruct((T, N), dt) for dt in out_dtypes],
        compiler_params=_cparams(("parallel", "parallel")),
        name=name,
    )(x, w)
    return outs


def _gate_body(x_ref, wb_ref, wa_ref, alog_ref, dtb_ref, beta_ref, g_ref):
    x = x_ref[...]
    b = _dot(x, wb_ref[...])
    a = _dot(x, wa_ref[...]) + dtb_ref[...]
    beta_ref[...] = 1.0 / (1.0 + jnp.exp(-b))
    softplus = jnp.maximum(a, 0.0) + jnp.log(1.0 + jnp.exp(-jnp.abs(a)))
    g_ref[...] = -jnp.exp(alog_ref[...]) * softplus


def _gate_proj(xn, w_b, w_a, a_log, dt_bias):
    T, K = xn.shape
    H = w_b.shape[1]
    pad = LANES - H
    wb = jnp.pad(w_b, ((0, 0), (0, pad))).astype(BF16)
    wa = jnp.pad(w_a, ((0, 0), (0, pad))).astype(BF16)
    al = jnp.pad(a_log.astype(F32), (0, pad)).reshape(1, LANES)
    db = jnp.pad(dt_bias.astype(F32), (0, pad)).reshape(1, LANES)
    tm = _tile(T, 1024, BF16_SUBLANES)
    row = pl.BlockSpec((tm, K), lambda i: (i, 0))
    wsp = pl.BlockSpec((K, LANES), lambda i: (0, 0))
    vsp = pl.BlockSpec((1, LANES), lambda i: (0, 0))
    osp = pl.BlockSpec((tm, LANES), lambda i: (i, 0))
    beta, g = pl.pallas_call(
        _gate_body,
        grid=(T // tm,),
        in_specs=[row, wsp, wsp, vsp, vsp],
        out_specs=[osp, osp],
        out_shape=[jax.ShapeDtypeStruct((T, LANES), F32)] * 2,
        compiler_params=_cparams(("parallel",)),
        name="gdn_gate_proj",
    )(xn, wb, wa, al, db)
    return beta[:, :H], g[:, :H]


def _convmm_body(*refs, G, W, stride, tm, pad, tps, glu):
    hist = (W - 1) * stride
    it = iter(refs)
    x_ref = next(it)
    w_refs = [next(it) for _ in range(G)]
    wc_refs = [next(it) for _ in range(G)]
    b_refs = [next(it) for _ in range(G)] if glu else None
    st_refs = [next(it) for _ in range(G)]
    out_ref = next(it)
    nst_refs = [next(it) for _ in range(G)]
    yscs = [next(it) for _ in range(G)]

    m = pl.program_id(1)
    first = (m % tps) == 0
    last = (m % tps) == tps - 1
    tn = out_ref.shape[1]
    sub = min(tn, V7X_MXU_COLS)
    for g in range(G):
        ysc = yscs[g]

        @pl.when(first)
        def _():
            ysc[pad - hist:pad, :] = st_refs[g][...]

        @pl.when(jnp.logical_not(first))
        def _():
            ysc[0:pad, :] = ysc[tm:tm + pad, :]

    def project(c):
        cols = slice(c * sub, (c + 1) * sub)
        for g in range(G):
            yscs[g][pad:pad + tm, cols] = _dot(x_ref[...], w_refs[g][:, cols])

    rc = math.gcd(tm, CONV_ROW_CHUNK)

    def conv_act(c):
        cols = slice(c * sub, (c + 1) * sub)
        for r0 in range(0, tm, rc):
            convs = []
            for g in range(G):
                acc = None
                for i in range(W):
                    off = pad - (W - 1 - i) * stride + r0
                    term = yscs[g][off:off + rc, cols] * wc_refs[g][i:i + 1, cols]
                    acc = term if acc is None else acc + term
                if glu:
                    acc = acc + b_refs[g][:, cols]
                convs.append(acc)
            res = _silu(convs[0]) * convs[1] if glu else _silu(convs[0])
            out_ref[r0:r0 + rc, cols] = res.astype(out_ref.dtype)

    n_sub = tn // sub
    project(0)
    for c in range(1, n_sub):
        project(c)
        conv_act(c - 1)
    conv_act(n_sub - 1)

    for g in range(G):
        @pl.when(last)
        def _():
            nst_refs[g][...] = yscs[g][pad + tm - hist:pad + tm, :]


def _conv_matmul(x, w, wc, bias, state, *, groups, width, stride, nseq, glu, name, tm_pref, tn_pref):
    T, K = x.shape
    W = wc.shape[0]
    Ng = width // groups
    hist = (W - 1) * stride
    rows = T // nseq
    tm = _tile(rows, tm_pref, BF16_SUBLANES)
    tn = _tile(Ng, tn_pref, V7X_MXU_COLS)
    tps = rows // tm
    pad = -(-hist // F32_SUBLANES) * F32_SUBLANES
    assert tm >= pad and state.shape == (nseq, hist, groups * Ng)
    nb = Ng // tn

    def col(g):
        return lambda n, m: (0, n + g * nb)

    def stcol(g):
        return lambda n, m: (m // tps, 0, n + g * nb)

    in_specs = [pl.BlockSpec((tm, K), lambda n, m: (m, 0))]
    args = [x]
    in_specs += [pl.BlockSpec((K, tn), col(g), pipeline_mode=pl.Buffered(1)) for g in range(groups)]
    args += [w] * groups
    in_specs += [pl.BlockSpec((W, tn), col(g)) for g in range(groups)]
    args += [wc.astype(F32)] * groups
    if glu:
        in_specs += [pl.BlockSpec((1, tn), col(g)) for g in range(groups)]
        args += [bias.astype(F32).reshape(1, -1)] * groups
    in_specs += [pl.BlockSpec((None, hist, tn), stcol(g)) for g in range(groups)]
    args += [state.astype(F32)] * groups
    out_specs = [pl.BlockSpec((tm, tn), lambda n, m: (m, n))]
    out_specs += [pl.BlockSpec((None, hist, tn), stcol(0)) for g in range(groups)]
    out_shape = [jax.ShapeDtypeStruct((T, Ng), BF16)]
    out_shape += [jax.ShapeDtypeStruct((nseq, hist, Ng), F32) for g in range(groups)]
    outs = pl.pallas_call(
        functools.partial(_convmm_body, G=groups, W=W, stride=stride, tm=tm, pad=pad, tps=tps, glu=glu),
        grid=(nb, T // tm),
        in_specs=in_specs,
        out_specs=out_specs,
        out_shape=out_shape,
        scratch_shapes=[pltpu.VMEM((pad + tm, tn), F32) for _ in range(groups)],
        compiler_params=_cparams(("parallel", "arbitrary")),
        name=name,
    )(*args)
    new_state = outs[1] if groups == 1 else jnp.concatenate(outs[1:], axis=-1)
    return outs[0], new_state


def _gdn_body(q_ref, k_ref, v_ref, z_ref, beta_ref, g_ref, s0_ref, gn_ref, o_ref, sout_ref, S,
              *, hb, rep, dk, dv, n_chunks):
    n = pl.program_id(2)

    @pl.when(n == 0)
    def _():
        S[...] = s0_ref[0]

    C = q_ref.shape[0]
    ri = lax.broadcasted_iota(jnp.int32, (C, C), 0)
    ci = lax.broadcasted_iota(jnp.int32, (C, C), 1)
    incl = ri >= ci
    strict = ri > ci
    eye = (ri == ci).astype(F32)
    gall = g_ref[...]
    Gc = _dot_hi(incl.astype(F32), gall)
    beta = beta_ref[...]
    gain = gn_ref[...]

    n_sq = int(round(math.log2(C))) - 1
    heads = range(hb)
    kn, qn, kk, qk = [], [], [], []
    for jk in range(hb // rep):
        kf = k_ref[:, jk * dk:(jk + 1) * dk].astype(F32)
        qf = q_ref[:, jk * dk:(jk + 1) * dk].astype(F32)
        kn.append(kf * lax.rsqrt(jnp.sum(kf * kf, axis=-1, keepdims=True) + L2_EPS))
        qn.append(qf * lax.rsqrt(jnp.sum(qf * qf, axis=-1, keepdims=True) + L2_EPS) * (dk ** -0.5))
        kb = kn[jk].astype(BF16)
        kk.append(_dot_nt(kb, kb))
        qk.append(_dot_nt(qn[jk].astype(BF16), kb))
    Gcol = [Gc[:, j:j + 1] for j in heads]
    bcol = [beta[:, j:j + 1] for j in heads]
    dec_incl, Nm = [], []
    for j in heads:
        Grow = jnp.sum(eye * Gcol[j], axis=0, keepdims=True)
        diff = Gcol[j] - Grow
        dec = jnp.where(incl, jnp.exp(jnp.where(incl, diff, 0.0)), 0.0)
        dec_incl.append(dec)
        Nm.append(-(bcol[j] * kk[j // rep] * jnp.where(strict, dec, 0.0)))
    Tm = [eye + Nm[j] for j in heads]
    Np = Nm
    for _ in range(n_sq):
        Np = [_dot_x4(Np[j], Np[j]) for j in heads]
        Tm = [Tm[j] + _dot_x4(Tm[j], Np[j]) for j in heads]
    eG = [jnp.exp(Gcol[j]) for j in heads]
    sol = []
    for j in heads:
        vf = v_ref[:, j * dv:(j + 1) * dv].astype(F32)
        rhs = jnp.concatenate([bcol[j] * vf, (bcol[j] * eG[j]) * kn[j // rep]], axis=1)
        sol.append(_dot_x4(Tm[j], rhs))
    Glast = [Gc[C - 1:C, j:j + 1] for j in heads]
    Sold = [S[j] for j in heads]
    Sb = [Sold[j].astype(BF16) for j in heads]
    wq, Pb, kd = [], [], []
    for j in heads:
        q_g = qn[j // rep] * eG[j]
        wq.append(jnp.concatenate([sol[j][:, dv:].astype(BF16), q_g.astype(BF16)], axis=0))
        Pb.append((qk[j // rep] * dec_incl[j]).astype(BF16))
        kd.append((kn[j // rep] * jnp.exp(Glast[j] - Gcol[j])).astype(BF16))
    wqS = [_dot(wq[j], Sb[j]) for j in heads]
    ub = [(sol[j][:, :dv] - wqS[j][:C]).astype(BF16) for j in heads]
    for j in heads:
        S[j] = jnp.exp(Glast[j]) * Sold[j] + _dot_tn(kd[j], ub[j])
    for j in heads:
        o = wqS[j][C:] + _dot(Pb[j], ub[j])
        on = o * lax.rsqrt(jnp.mean(o * o, axis=-1, keepdims=True) + NORM_EPS) * gain
        o_ref[:, j * dv:(j + 1) * dv] = (on * z_ref[:, j * dv:(j + 1) * dv].astype(F32)).astype(o_ref.dtype)

    @pl.when(n == n_chunks - 1)
    def _():
        sout_ref[0] = S[...]


def _gdn_core(qkv, sz, beta, g, S0, out_norm, *, nseq, chunk, hb):
    T = qkv.shape[0]
    _, HV, dk, dv = S0.shape
    VW = HV * dv
    QK = (qkv.shape[1] - VW) // 2
    HK = QK // dk
    rep = HV // HK
    HG = HV // hb
    kb = hb // rep
    L = T // nseq
    n_chunks = L // chunk

    def group_major(t):
        t = t.reshape(T, HG, hb).transpose(1, 0, 2)
        return jnp.pad(t, ((0, 0), (0, 0), (0, LANES - hb)))

    rowblk = lambda b, h, n: b * n_chunks + n
    in_specs = [
        pl.BlockSpec((chunk, kb * dk), lambda b, h, n: (rowblk(b, h, n), h)),
        pl.BlockSpec((chunk, kb * dk), lambda b, h, n: (rowblk(b, h, n), HG + h)),
        pl.BlockSpec((chunk, hb * dv), lambda b, h, n: (rowblk(b, h, n), 2 * QK // (hb * dv) + h)),
        pl.BlockSpec((chunk, hb * dv), lambda b, h, n: (rowblk(b, h, n), h)),
        pl.BlockSpec((None, chunk, LANES), lambda b, h, n: (h, rowblk(b, h, n), 0)),
        pl.BlockSpec((None, chunk, LANES), lambda b, h, n: (h, rowblk(b, h, n), 0)),
        pl.BlockSpec((1, hb, dk, dv), lambda b, h, n: (b, h, 0, 0)),
        pl.BlockSpec((1, dv), lambda b, h, n: (0, 0)),
    ]
    out_specs = [
        pl.BlockSpec((chunk, hb * dv), lambda b, h, n: (rowblk(b, h, n), h)),
        pl.BlockSpec((1, hb, dk, dv), lambda b, h, n: (b, h, 0, 0)),
    ]
    o, S = pl.pallas_call(
        functools.partial(_gdn_body, hb=hb, rep=rep, dk=dk, dv=dv, n_chunks=n_chunks),
        grid=(nseq, HG, n_chunks),
        in_specs=in_specs,
        out_specs=out_specs,
        out_shape=[jax.ShapeDtypeStruct((T, VW), BF16), jax.ShapeDtypeStruct(S0.shape, F32)],
        scratch_shapes=[pltpu.VMEM((hb, dk, dv), F32)],
        compiler_params=_cparams(("parallel", "parallel", "arbitrary")),
        name="gdn_core",
    )(qkv, qkv, qkv, sz, group_major(beta), group_major(g), S0.astype(F32),
      out_norm.astype(F32).reshape(1, dv))
    return o, S


def _mmres_body(x_ref, w_ref, res_ref, *refs, n_norm, with_h):
    gains = refs[:n_norm]
    outs = refs[n_norm:]
    h = res_ref[...] + _dot(x_ref[...], w_ref[...])
    k = 0
    if with_h:
        outs[0][...] = h
        k = 1
    if n_norm:
        inv = lax.rsqrt(jnp.mean(h * h, axis=-1, keepdims=True) + NORM_EPS)
        y = h * inv
        for i in range(n_norm):
            outs[k + i][...] = (y * gains[i][...]).astype(outs[k + i].dtype)


def _matmul_residual(x, w, res, gains, norm_dtype, with_h, name):
    T, K = x.shape
    D = w.shape[1]
    tm = _tile(T, 256, BF16_SUBLANES)
    row = lambda i: (i, 0)
    fixed = lambda i: (0, 0)
    n_norm = len(gains)
    in_specs = [pl.BlockSpec((tm, K), row),
                pl.BlockSpec((K, D), fixed, pipeline_mode=pl.Buffered(1)),
                pl.BlockSpec((tm, D), row)]
    in_specs += [pl.BlockSpec((1, D), fixed) for _ in gains]
    out_shape = ([jax.ShapeDtypeStruct((T, D), F32)] if with_h else []) \
        + [jax.ShapeDtypeStruct((T, D), norm_dtype) for _ in gains]
    outs = pl.pallas_call(
        functools.partial(_mmres_body, n_norm=n_norm, with_h=with_h),
        grid=(T // tm,),
        in_specs=in_specs,
        out_specs=[pl.BlockSpec((tm, D), row) for _ in out_shape],
        out_shape=out_shape,
        compiler_params=_cparams(("parallel",)),
        name=name,
    )(x, w, res, *[g.astype(F32).reshape(1, D) for g in gains])
    return outs


def _bucket_table(num_buckets):
    half = num_buckets // 2
    exact = half // 2
    n = np.arange(MAX_DISTANCE + 1)
    large = exact + (np.log(np.maximum(n, 1).astype(np.float32) / np.float32(exact))
                     / np.float32(math.log(MAX_DISTANCE / exact)) * np.float32(half - exact)).astype(np.int32)
    large = np.minimum(large, half - 1)
    return np.where(n < exact, n, large), half


def _bias_body(rb_ref, out_ref, *, tiles, steps, half):
    h = pl.program_id(0)
    for t, (q0, k0) in enumerate(tiles):
        nq, nk = out_ref.shape[2], out_ref.shape[3]
        qpos = q0 + lax.broadcasted_iota(jnp.int32, (nq, nk), 0)
        kpos = k0 + lax.broadcasted_iota(jnp.int32, (nq, nk), 1)
        rel = kpos - qpos
        n = jnp.abs(rel)
        neg = jnp.full((nq, nk), rb_ref[0, h], F32)
        pos = jnp.full((nq, nk), rb_ref[half, h], F32)
        for thr, b in steps:
            ge = n >= thr
            neg = jnp.where(ge, rb_ref[b, h], neg)
            pos = jnp.where(ge, rb_ref[half + b, h], pos)
        bias = jnp.where(rel > 0, pos, neg)
        visible = kpos // CHUNK <= qpos // CHUNK
        out_ref[t, 0] = jnp.where(visible, bias, NEG_INF)


def _bias_tiles(rel_bias, tiles, nq, nk):
    NB, H = rel_bias.shape
    table, half = _bucket_table(NB)
    steps = [(int(i), int(table[i])) for i in range(1, len(table)) if table[i] != table[i - 1]]
    return pl.pallas_call(
        functools.partial(_bias_body, tiles=tuple(tiles), steps=tuple(steps), half=half),
        grid=(H,),
        in_specs=[pl.BlockSpec(memory_space=pltpu.SMEM)],
        out_specs=pl.BlockSpec((len(tiles), 1, nq, nk), lambda h: (0, h, 0, 0)),
        out_shape=jax.ShapeDtypeStruct((len(tiles), H, nq, nk), F32),
        compiler_params=_cparams(("parallel",)),
        name="t5_bias_tiles",
    )(rel_bias.astype(F32))


def _lane_tile(x, n):
    if n <= LANES:
        return x[:, :n]
    return jnp.concatenate([x] * (n // LANES), axis=1)


def _softmax_update(s, vb, m_ref, l_ref, acc_ref, idx):
    m_prev = m_ref[idx]
    l_prev = l_ref[idx]
    m_new = jnp.maximum(m_prev, jnp.max(s, axis=1, keepdims=True))
    alpha = jnp.exp(m_prev - m_new)
    p = jnp.exp(s - _lane_tile(m_new, s.shape[1]))
    l_ref[idx] = alpha * l_prev + jnp.sum(p, axis=1, keepdims=True)
    acc_ref[idx] = acc_ref[idx] * _lane_tile(alpha, vb.shape[1]) + _dot(p.astype(BF16), vb)
    m_ref[idx] = m_new


def _diff_finalize(lam_ref, sn_ref, o_ref, l_ref, acc_ref, *, H, d, lam_init):
    lv = lam_ref[...]
    lam = (jnp.exp(jnp.sum(lv[0:1] * lv[1:2], axis=1, keepdims=True))
           - jnp.exp(jnp.sum(lv[2:3] * lv[3:4], axis=1, keepdims=True)) + lam_init)
    for h in range(H):
        o = (acc_ref[2 * h] / _lane_tile(l_ref[2 * h], 2 * d)
             - lam * (acc_ref[2 * h + 1] / _lane_tile(l_ref[2 * h + 1], 2 * d)))
        on = o * lax.rsqrt(jnp.mean(o * o, axis=-1, keepdims=True) + DIFF_SUBLN_EPS) * sn_ref[...]
        o_ref[:, h * 2 * d:(h + 1) * 2 * d] = (on * (1.0 - lam_init)).astype(o_ref.dtype)


def _attn_body(qi_ref, kj_ref, q_ref, k_ref, v_ref, bias_ref, lam_ref, sn_ref, o_ref, m_ref, l_ref, acc_ref,
               *, H, d, lam_init):
    p = pl.program_id(1)
    qi = qi_ref[p]
    kj = kj_ref[p]

    @pl.when(kj == 0)
    def _():
        m_ref[...] = jnp.full(m_ref.shape, -jnp.inf, F32)
        l_ref[...] = jnp.zeros(l_ref.shape, F32)
        acc_ref[...] = jnp.zeros(acc_ref.shape, F32)

    bidx = jnp.minimum(qi - kj, 2)
    for h in range(H):
        vb = v_ref[:, h * 2 * d:(h + 1) * 2 * d]
        for mp in range(2):
            c0 = (2 * h + mp) * d
            s = _dot_nt(q_ref[:, c0:c0 + d], k_ref[:, c0:c0 + d]) + bias_ref[bidx, h]
            _softmax_update(s, vb, m_ref, l_ref, acc_ref, 2 * h + mp)

    @pl.when(kj == qi)
    def _():
        _diff_finalize(lam_ref, sn_ref, o_ref, l_ref, acc_ref, H=H, d=d, lam_init=lam_init)


def _diff_attention_prompt(q, k, v, rel_bias, lam_vecs, sub_norm, lam_init, *, nseq, d):
    T, QW = q.shape
    H = QW // (2 * d)
    L = T // nseq
    tq = _tile(L, 256, LANES)
    assert tq % CHUNK == 0 and tq >= MAX_DISTANCE and L % tq == 0
    nq = L // tq
    pairs = [(i, j) for i in range(nq) for j in range(i + 1)]
    qi = jnp.asarray([pq for pq, _ in pairs], jnp.int32)
    kj = jnp.asarray([pk for _, pk in pairs], jnp.int32)
    bias = _bias_tiles(rel_bias, [(0, 0), (tq, 0), (2 * tq, 0)], tq, tq)
    qrow = lambda b, p, qi_r, kj_r: (b * nq + qi_r[p], 0)
    krow = lambda b, p, qi_r, kj_r: (b * nq + kj_r[p], 0)
    fixed2 = lambda b, p, qi_r, kj_r: (0, 0)
    grid_spec = pltpu.PrefetchScalarGridSpec(
        num_scalar_prefetch=2,
        grid=(nseq, len(pairs)),
        in_specs=[
            pl.BlockSpec((tq, QW), qrow),
            pl.BlockSpec((tq, QW), krow),
            pl.BlockSpec((tq, QW), krow),
            pl.BlockSpec((3, H, tq, tq), lambda b, p, qi_r, kj_r: (0, 0, 0, 0), pipeline_mode=pl.Buffered(1)),
            pl.BlockSpec((4, d), fixed2),
            pl.BlockSpec((1, 2 * d), fixed2),
        ],
        out_specs=pl.BlockSpec((tq, QW), qrow),
        scratch_shapes=[pltpu.VMEM((2 * H, tq, LANES), F32), pltpu.VMEM((2 * H, tq, LANES), F32),
                        pltpu.VMEM((2 * H, tq, 2 * d), F32)],
    )
    return pl.pallas_call(
        functools.partial(_attn_body, H=H, d=d, lam_init=lam_init),
        grid_spec=grid_spec,
        out_shape=jax.ShapeDtypeStruct((T, QW), BF16),
        compiler_params=_cparams(("parallel", "arbitrary")),
        name="diff_attention_prompt",
    )(qi, kj, q, k, v, bias, lam_vecs, sub_norm.astype(F32).reshape(1, 2 * d))


def _attn_dec_body(q_ref, ck_ref, cv_ref, kn_ref, vn_ref, bc_ref, bn_ref, lam_ref, sn_ref, o_ref,
                   m_ref, l_ref, acc_ref, *, H, d, lam_init, n_blocks):
    kb_i = pl.program_id(1)

    @pl.when(kb_i == 0)
    def _():
        m_ref[...] = jnp.full(m_ref.shape, -jnp.inf, F32)
        l_ref[...] = jnp.zeros(l_ref.shape, F32)
        acc_ref[...] = jnp.zeros(acc_ref.shape, F32)

    def sweep(k_all, v_all, b_ref):
        for h in range(H):
            vb = v_all[:, h * 2 * d:(h + 1) * 2 * d]
            for mp in range(2):
                c0 = (2 * h + mp) * d
                s = _dot_nt(q_ref[:, c0:c0 + d], k_all[:, c0:c0 + d]) + b_ref[h]
                _softmax_update(s, vb, m_ref, l_ref, acc_ref, 2 * h + mp)

    sweep(ck_ref[0].astype(BF16), cv_ref[0].astype(BF16), bc_ref)

    @pl.when(kb_i == n_blocks - 1)
    def _():
        sweep(kn_ref[...], vn_ref[...], bn_ref)
        _diff_finalize(lam_ref, sn_ref, o_ref, l_ref, acc_ref, H=H, d=d, lam_init=lam_init)


def _diff_attention_decode(q, k_new, v_new, cache_k, cache_v, rel_bias, lam_vecs, sub_norm, lam_init, *, d):
    B, P, QW = cache_k.shape
    H = QW // (2 * d)
    Lq = q.shape[0] // B
    tk = _tile(P, 512, LANES)
    nb = P // tk
    bias_c = _bias_tiles(rel_bias, [(P, j * tk) for j in range(nb)], Lq, tk)
    bias_n = _bias_tiles(rel_bias, [(P, P)], Lq, Lq)[0]
    seq = lambda b, j: (b, 0)
    fixed2 = lambda b, j: (0, 0)
    return pl.pallas_call(
        functools.partial(_attn_dec_body, H=H, d=d, lam_init=lam_init, n_blocks=nb),
        grid=(B, nb),
        in_specs=[
            pl.BlockSpec((Lq, QW), seq),
            pl.BlockSpec((1, tk, QW), lambda b, j: (b, j, 0)),
            pl.BlockSpec((1, tk, QW), lambda b, j: (b, j, 0)),
            pl.BlockSpec((Lq, QW), seq),
            pl.BlockSpec((Lq, QW), seq),
            pl.BlockSpec((None, H, Lq, tk), lambda b, j: (j, 0, 0, 0)),
            pl.BlockSpec((H, Lq, Lq), lambda b, j: (0, 0, 0)),
            pl.BlockSpec((4, d), fixed2),
            pl.BlockSpec((1, 2 * d), fixed2),
        ],
        out_specs=pl.BlockSpec((Lq, QW), seq),
        out_shape=jax.ShapeDtypeStruct((B * Lq, QW), BF16),
        scratch_shapes=[pltpu.VMEM((2 * H, Lq, LANES), F32), pltpu.VMEM((2 * H, Lq, LANES), F32),
                        pltpu.VMEM((2 * H, Lq, 2 * d), F32)],
        compiler_params=_cparams(("parallel", "arbitrary")),
        name="diff_attention_decode",
    )(q, cache_k, cache_v, k_new, v_new, bias_c, bias_n, lam_vecs, sub_norm.astype(F32).reshape(1, 2 * d))


def _trunk(x, p, st_gdn, st_gdn_conv, st_ffn_conv, cache_k, cache_v):
    B, L, D = x.shape
    fresh = cache_k is None
    T = B * L
    depth = p['f_norm'].shape[0]
    n_a = p['a_norm'].shape[0]
    HV, dv = p['a_log'].shape[1], p['a_out_norm'].shape[1]
    VW = HV * dv
    QKVW = p['a_w_conv'].shape[2]
    dk = p['gdn_head_k']
    d = p['b_lam_q1'].shape[1]
    DFF = p['f_w_down'].shape[1]
    QW = p['b_w_q'].shape[2]

    time_major = not fresh
    if time_major:
        h = x.transpose(1, 0, 2).reshape(T, D)
        stride, nseq_conv = B, 1
    else:
        h = x.reshape(T, D)
        stride, nseq_conv = 1, B

    def to_seq(t):
        return t.reshape(L, B, -1).transpose(1, 0, 2).reshape(T, -1) if time_major else t

    def to_time(t):
        return t.reshape(B, L, -1).transpose(1, 0, 2).reshape(T, -1) if time_major else t

    def conv_state_in(st, width, chans):
        if st is None:
            return jnp.zeros((nseq_conv, (width - 1) * stride, chans), F32)
        return st.transpose(1, 0, 2).reshape(1, (width - 1) * B, chans)

    def conv_state_out(st, width):
        if time_major:
            return st.reshape(width - 1, B, -1).transpose(1, 0, 2)
        return st

    new_S, new_gconv, new_fconv = [], [], []
    k_f32 = v_f32 = k_bf = v_bf = None
    xn = _rmsnorm_cast(h, p['a_norm'][0]) if n_a > 0 else _rmsnorm_cast(h, p['b_norm'][0])
    xn_kv = None
    y = None
    for layer in range(depth):
        if layer < n_a:
            i = layer
            w_in = p['a_w_in_bf'][i]
            w_ba = p['a_w_in'][i][:, QKVW + VW:]
            gw = p['a_w_conv'].shape[1]
            cst = conv_state_in(None if fresh else st_gdn_conv[i], gw, QKVW)
            qkv, gconv = _conv_matmul(xn, w_in, p['a_w_conv'][i], None, cst, groups=1, width=QKVW,
                                      stride=stride, nseq=nseq_conv, glu=False, name="gdn_qkv_conv",
                                      tm_pref=512, tn_pref=2048)
            (sz,) = _matmul(xn, w_in, lambda acc: (_silu(acc),), [BF16], "gdn_z_proj", col0=QKVW, width=VW)
            beta, g = _gate_proj(xn, w_ba[:, :HV], w_ba[:, HV:], p['a_log'][i], p['a_dt_bias'][i])
            S0 = jnp.zeros((B, HV, dk, dv), F32) if fresh else st_gdn[i]
            og, S = _gdn_core(to_seq(qkv), to_seq(sz), to_seq(beta), to_seq(g), S0, p['a_out_norm'][i],
                              nseq=B, chunk=min(CHUNK, L), hb=min(8, HV))
            new_S.append(S)
            new_gconv.append(conv_state_out(gconv, gw))
            h, xn = _matmul_residual(to_time(og), p['a_w_out_bf'][i], h, [p['f_norm'][layer]],
                                     BF16, True, "gdn_out_proj")
        else:
            j = layer - n_a
            lam_init = 0.8 - 0.6 * math.exp(-0.3 * layer)
            scale = d ** -0.5
            (q,) = _matmul(xn, p['b_w_q_bf'][j], lambda acc: (acc * scale,), [BF16], "diff_q_proj")
            lam_vecs = jnp.stack([p['b_lam_q1'][j], p['b_lam_k1'][j], p['b_lam_q2'][j],
                                  p['b_lam_k2'][j]]).astype(F32)
            if fresh:
                ao = _diff_attention_prompt(q, k_bf, v_bf, p['rel_bias'], lam_vecs, p['b_sub_norm'][j],
                                            lam_init, nseq=B, d=d)
            else:
                ao = _diff_attention_decode(to_seq(q), to_seq(k_bf), to_seq(v_bf),
                                            cache_k.reshape(B, -1, QW), cache_v.reshape(B, -1, QW),
                                            p['rel_bias'], lam_vecs, p['b_sub_norm'][j], lam_init, d=d)
                ao = to_time(ao)
            h, xn = _matmul_residual(ao, p['b_w_o_bf'][j], h, [p['f_norm'][layer]],
                                     BF16, True, "diff_out_proj")
        fw = p['f_w_conv'].shape[1]
        fst = conv_state_in(None if fresh else st_ffn_conv[layer], fw, 2 * DFF)
        act, fconv = _conv_matmul(xn, p['f_w_up_bf'][layer], p['f_w_conv'][layer],
                                  p['f_b_conv'][layer], fst, groups=2, width=2 * DFF, stride=stride,
                                  nseq=nseq_conv, glu=True, name="ffn_up_conv", tm_pref=512, tn_pref=2816)
        new_fconv.append(conv_state_out(fconv, fw))
        last = layer == depth - 1
        gains = []
        if layer == n_a - 1:
            gains.append(p['kv_norm'])
        if last:
            gains.append(p['final_norm'])
        elif layer + 1 < n_a:
            gains.append(p['a_norm'][layer + 1])
        else:
            gains.append(p['b_norm'][layer + 1 - n_a])
        outs = _matmul_residual(act, p['f_w_down_bf'][layer], h, gains,
                                F32 if last else BF16, not last, "ffn_down_proj")
        if last:
            if layer == n_a - 1:
                xn_kv = outs[0].astype(BF16)
            y = outs[-1]
        else:
            h = outs[0]
            if layer == n_a - 1:
                xn_kv = outs[1]
            xn = outs[-1]
        if layer == n_a - 1:
            both = lambda acc: (acc, acc)
            k_f32, k_bf = _matmul(xn_kv, p['w_kv_bf'], both, [F32, BF16], "k_proj", col0=0, width=QW)
            v_f32, v_bf = _matmul(xn_kv, p['w_kv_bf'], both, [F32, BF16], "v_proj", col0=QW, width=QW)

    n_kh = 2 * (QW // (2 * d))
    y = to_seq(y).reshape(B, L, D)
    k_sh = to_seq(k_f32).reshape(B, L, n_kh, d)
    v_sh = to_seq(v_f32).reshape(B, L, n_kh // 2, 2 * d)
    return y, jnp.stack(new_S), jnp.stack(new_gconv), jnp.stack(new_fconv), k_sh, v_sh


def kernel(x_prompt, x_sample, state_gdn, state_gdn_conv, state_ffn_conv, cache_k, cache_v, a_norm, a_w_in, a_w_conv, a_log, a_dt_bias, a_out_norm, a_w_out, kv_norm, w_kv, b_norm, b_w_q, b_lam_q1, b_lam_k1, b_lam_q2, b_lam_k2, b_sub_norm, b_w_o, rel_bias, f_norm, f_w_up, f_w_conv, f_b_conv, f_w_down, final_norm):
    p = {
        'a_norm': a_norm, 'a_w_in': a_w_in, 'a_w_conv': a_w_conv, 'a_log': a_log,
        'a_dt_bias': a_dt_bias, 'a_out_norm': a_out_norm, 'a_w_out': a_w_out,
        'kv_norm': kv_norm, 'w_kv': w_kv,
        'b_norm': b_norm, 'b_w_q': b_w_q, 'b_lam_q1': b_lam_q1, 'b_lam_k1': b_lam_k1,
        'b_lam_q2': b_lam_q2, 'b_lam_k2': b_lam_k2, 'b_sub_norm': b_sub_norm, 'b_w_o': b_w_o,
        'rel_bias': rel_bias,
        'f_norm': f_norm, 'f_w_up': f_w_up, 'f_w_conv': f_w_conv, 'f_b_conv': f_b_conv,
        'f_w_down': f_w_down, 'final_norm': final_norm,
        'gdn_head_k': state_gdn.shape[3],
    }
    for name in ('a_w_in', 'a_w_out', 'w_kv', 'b_w_q', 'b_w_o', 'f_w_up', 'f_w_down'):
        p[name + '_bf'] = p[name].astype(BF16)
    out_p = _trunk(x_prompt, p, None, None, None, None, None)
    out_s = _trunk(x_sample, p, state_gdn, state_gdn_conv, state_ffn_conv, cache_k, cache_v)
    return (out_p[0], out_s[0]) + tuple(out_p[1:]) + tuple(out_s[1:])
```

```python
import functools
import math

import numpy as np
import jax
import jax.numpy as jnp
from jax import lax
from jax.experimental import pallas as pl
from jax.experimental.pallas import tpu as pltpu

F32 = jnp.float32
BF16 = jnp.bfloat16

CHUNK = 64
NORM_EPS = 1e-6
DIFF_SUBLN_EPS = 1e-5
L2_EPS = 1e-6
MAX_DISTANCE = 128
NEG_INF = -1e30
GDN_HEADS_PER_STEP = 16

V7X_VMEM_LIMIT = 56 * 1024 * 1024
LANES = 128
V7X_MXU_COLS = 256
F32_SUBLANES = 8
BF16_SUBLANES = 16


def _cparams(sem):
    return pltpu.CompilerParams(dimension_semantics=sem, vmem_limit_bytes=V7X_VMEM_LIMIT)


def _tile(n, pref, mult):
    if n <= pref:
        return n
    t = (pref // mult) * mult
    while t >= mult:
        if n % t == 0:
            return t
        t -= mult
    return n


def _dot(a, b):
    return jnp.dot(a, b, preferred_element_type=F32)


def _dot_nt(a, b):
    return lax.dot_general(a, b, (((1,), (1,)), ((), ())), preferred_element_type=F32)


def _dot_tn(a, b):
    return lax.dot_general(a, b, (((0,), (0,)), ((), ())), preferred_element_type=F32)


def _dot_hi(a, b):
    return jnp.dot(a, b, preferred_element_type=F32, precision=lax.Precision.HIGHEST)


def _split_bf16(x):
    hi = x.astype(BF16).astype(F32)
    return hi, x - hi


def _x4_lhs(x, parts=None):
    xh, xl = parts or _split_bf16(x)
    xx = jnp.concatenate([xh, xl], axis=1).astype(BF16)
    return jnp.concatenate([xx, xx], axis=1)


def _x4_rhs(y, parts=None):
    yh, yl = parts or _split_bf16(y)
    return jnp.concatenate([yh, yh, yl, yl], axis=0).astype(BF16)


def _x4_both(x):
    parts = _split_bf16(x)
    return _x4_lhs(x, parts), _x4_rhs(x, parts)


def _dot_x4(lhs4, rhs4):
    return _dot(lhs4, rhs4)


def _silu(x):
    h = 0.5 * x
    return h + h * jnp.tanh(h)


def _rms_body(x_ref, g_ref, o_ref):
    x = x_ref[...]
    ms = jnp.mean(x * x, axis=-1, keepdims=True)
    o_ref[...] = (x * lax.rsqrt(ms + NORM_EPS) * g_ref[...]).astype(o_ref.dtype)


def _rmsnorm_cast(x, g):
    T, D = x.shape
    tm = _tile(T, 512, BF16_SUBLANES)
    return pl.pallas_call(
        _rms_body,
        grid=(T // tm,),
        in_specs=[pl.BlockSpec((tm, D), lambda i: (i, 0)), pl.BlockSpec((1, D), lambda i: (0, 0))],
        out_specs=pl.BlockSpec((tm, D), lambda i: (i, 0)),
        out_shape=jax.ShapeDtypeStruct((T, D), BF16),
        compiler_params=_cparams(("parallel",)),
        name="rmsnorm_cast",
    )(x, g.reshape(1, D).astype(F32))


def _mm_body(x_ref, w_ref, *o_refs, epilogue):
    acc = _dot(x_ref[...], w_ref[...])
    for o_ref, val in zip(o_refs, epilogue(acc)):
        o_ref[...] = val.astype(o_ref.dtype)


def _matmul(x, w, epilogue, out_dtypes, name, col0=0, width=None, tm_pref=1024, tn_pref=1024):
    T, K = x.shape
    N = w.shape[1] if width is None else width
    tm = _tile(T, tm_pref, BF16_SUBLANES)
    tn = _tile(math.gcd(N, col0) if col0 else N, tn_pref, LANES)
    cb = col0 // tn
    outs = pl.pallas_call(
        functools.partial(_mm_body, epilogue=epilogue),
        grid=(N // tn, T // tm),
        in_specs=[pl.BlockSpec((tm, K), lambda n, m: (m, 0)), pl.BlockSpec((K, tn), lambda n, m: (0, n + cb))],
        out_specs=[pl.BlockSpec((tm, tn), lambda n, m: (m, n)) for _ in out_dtypes],
        out_shape=[jax.ShapeDtypeStruct((T, N), dt) for dt in out_dtypes],
        compiler_params=_cparams(("parallel", "parallel")),
        name=name,
    )(x, w)
    return outs


def _headproj_body(x_ref, w_ref, o3_ref, obf_ref, *, hd):
    acc = _dot(x_ref[...], w_ref[...])
    obf_ref[...] = acc.astype(obf_ref.dtype)
    for hh in range(o3_ref.shape[1]):
        o3_ref[:, hh, :] = acc[:, hh * hd:(hh + 1) * hd]


def _head_projection(x, w, col0, width, hd, name):
    T, K = x.shape
    nh = width // hd
    hb = nh if nh <= F32_SUBLANES else F32_SUBLANES
    tn = hb * hd
    tm = _tile(T, 512, BF16_SUBLANES)
    cb = col0 // tn
    return pl.pallas_call(
        functools.partial(_headproj_body, hd=hd),
        grid=(width // tn, T // tm),
        in_specs=[pl.BlockSpec((tm, K), lambda n, m: (m, 0)), pl.BlockSpec((K, tn), lambda n, m: (0, n + cb))],
        out_specs=[pl.BlockSpec((tm, hb, hd), lambda n, m: (m, n, 0)), pl.BlockSpec((tm, tn), lambda n, m: (m, n))],
        out_shape=[jax.ShapeDtypeStruct((T, nh, hd), F32), jax.ShapeDtypeStruct((T, width), BF16)],
        compiler_params=_cparams(("parallel", "parallel")),
        name=name,
    )(x, w)


def _gate_body(x_ref, wb_ref, wa_ref, alog_ref, dtb_ref, beta_ref, g_ref):
    x = x_ref[...]
    b = _dot(x, wb_ref[...])
    a = _dot(x, wa_ref[...]) + dtb_ref[...]
    beta_ref[...] = 1.0 / (1.0 + jnp.exp(-b))
    softplus = jnp.maximum(a, 0.0) + jnp.log(1.0 + jnp.exp(-jnp.abs(a)))
    g_ref[...] = -jnp.exp(alog_ref[...]) * softplus


def _gate_proj(xn, w_b, w_a, a_log, dt_bias):
    T, K = xn.shape
    H = w_b.shape[1]
    pad = LANES - H
    wb = jnp.pad(w_b, ((0, 0), (0, pad))).astype(BF16)
    wa = jnp.pad(w_a, ((0, 0), (0, pad))).astype(BF16)
    al = jnp.pad(a_log.astype(F32), (0, pad)).reshape(1, LANES)
    db = jnp.pad(dt_bias.astype(F32), (0, pad)).reshape(1, LANES)
    tm = _tile(T, 1024, BF16_SUBLANES)
    row = pl.BlockSpec((tm, K), lambda i: (i, 0))
    wsp = pl.BlockSpec((K, LANES), lambda i: (0, 0))
    vsp = pl.BlockSpec((1, LANES), lambda i: (0, 0))
    osp = pl.BlockSpec((tm, LANES), lambda i: (i, 0))
    beta, g = pl.pallas_call(
        _gate_body,
        grid=(T // tm,),
        in_specs=[row, wsp, wsp, vsp, vsp],
        out_specs=[osp, osp],
        out_shape=[jax.ShapeDtypeStruct((T, LANES), F32)] * 2,
        compiler_params=_cparams(("parallel",)),
        name="gdn_gate_proj",
    )(xn, wb, wa, al, db)
    return beta[:, :H], g[:, :H]


def _convmm_body(*refs, G, W, stride, tm, pad, tps, glu, n_sub):
    hist = (W - 1) * stride
    it = iter(refs)
    x_ref = next(it)
    w_refs = [next(it) for _ in range(G)]
    wc_refs = [next(it) for _ in range(G)]
    b_refs = [next(it) for _ in range(G)] if glu else None
    st_refs = [next(it) for _ in range(G)]
    out_ref = next(it)
    nst_refs = [next(it) for _ in range(G)]
    yscs = [[next(it) for _ in range(n_sub)] for _ in range(G)]

    m = pl.program_id(1)
    first = (m % tps) == 0
    last = (m % tps) == tps - 1
    sub = out_ref.shape[1] // n_sub

    @pl.when(first)
    def _():
        for g in range(G):
            for c in range(n_sub):
                yscs[g][c][pad - hist:pad, :] = st_refs[g][:, c * sub:(c + 1) * sub]

    @pl.when(jnp.logical_not(first))
    def _():
        for g in range(G):
            for c in range(n_sub):
                yscs[g][c][0:pad, :] = yscs[g][c][tm:tm + pad, :]

    def project(c):
        cols = slice(c * sub, (c + 1) * sub)
        for g in range(G):
            yscs[g][c][pad:pad + tm, :] = _dot(x_ref[...], w_refs[g][:, cols])

    def conv_act(c):
        cols = slice(c * sub, (c + 1) * sub)
        convs = []
        for g in range(G):
            acc = None
            for i in range(W):
                off = pad - (W - 1 - i) * stride
                term = yscs[g][c][off:off + tm, :] * wc_refs[g][i:i + 1, cols]
                acc = term if acc is None else acc + term
            if glu:
                acc = acc + b_refs[g][:, cols]
            convs.append(acc)
        res = _silu(convs[0]) * convs[1] if glu else _silu(convs[0])
        out_ref[:, cols] = res.astype(out_ref.dtype)

    project(0)
    for c in range(1, n_sub):
        project(c)
        conv_act(c - 1)
    conv_act(n_sub - 1)

    @pl.when(last)
    def _():
        for g in range(G):
            for c in range(n_sub):
                nst_refs[g][:, c * sub:(c + 1) * sub] = yscs[g][c][pad + tm - hist:pad + tm, :]


def _conv_matmul(x, w, wc, bias, state, *, groups, width, stride, nseq, glu, name, tm_pref, tn_pref):
    T, K = x.shape
    W = wc.shape[0]
    Ng = width // groups
    hist = (W - 1) * stride
    rows = T // nseq
    tm = _tile(rows, tm_pref, BF16_SUBLANES)
    tn = _tile(Ng, tn_pref, V7X_MXU_COLS)
    n_sub = max(tn // V7X_MXU_COLS, 1)
    tps = rows // tm
    pad = -(-hist // F32_SUBLANES) * F32_SUBLANES
    assert tm >= pad and state.shape == (nseq, hist, groups * Ng)
    nb = Ng // tn

    def col(g):
        return lambda n, m: (0, n + g * nb)

    def stcol(g):
        return lambda n, m: (m // tps, 0, n + g * nb)

    in_specs = [pl.BlockSpec((tm, K), lambda n, m: (m, 0))]
    args = [x]
    in_specs += [pl.BlockSpec((K, tn), col(g), pipeline_mode=pl.Buffered(1)) for g in range(groups)]
    args += [w] * groups
    in_specs += [pl.BlockSpec((W, tn), col(g)) for g in range(groups)]
    args += [wc.astype(F32)] * groups
    if glu:
        in_specs += [pl.BlockSpec((1, tn), col(g)) for g in range(groups)]
        args += [bias.astype(F32).reshape(1, -1)] * groups
    in_specs += [pl.BlockSpec((None, hist, tn), stcol(g)) for g in range(groups)]
    args += [state.astype(F32)] * groups
    out_specs = [pl.BlockSpec((tm, tn), lambda n, m: (m, n))]
    out_specs += [pl.BlockSpec((None, hist, tn), stcol(0)) for g in range(groups)]
    out_shape = [jax.ShapeDtypeStruct((T, Ng), BF16)]
    out_shape += [jax.ShapeDtypeStruct((nseq, hist, Ng), F32) for g in range(groups)]
    outs = pl.pallas_call(
        functools.partial(_convmm_body, G=groups, W=W, stride=stride, tm=tm, pad=pad, tps=tps, glu=glu,
                          n_sub=n_sub),
        grid=(nb, T // tm),
        in_specs=in_specs,
        out_specs=out_specs,
        out_shape=out_shape,
        scratch_shapes=[pltpu.VMEM((pad + tm, tn // n_sub), F32) for _ in range(groups * n_sub)],
        compiler_params=_cparams(("parallel", "arbitrary")),
        name=name,
    )(*args)
    new_state = outs[1] if groups == 1 else jnp.concatenate(outs[1:], axis=-1)
    return outs[0], new_state


def _gdn_body(q_ref, k_ref, v_ref, z_ref, beta_ref, g_ref, s0_ref, gn_ref, o_ref, sout_ref, S,
              *, hb, rep, dk, dv, n_chunks):
    n = pl.program_id(2)

    @pl.when(n == 0)
    def _():
        S[...] = s0_ref[0]

    C = q_ref.shape[0]
    ri = lax.broadcasted_iota(jnp.int32, (C, C), 0)
    ci = lax.broadcasted_iota(jnp.int32, (C, C), 1)
    incl = ri >= ci
    strict = ri > ci
    eye = (ri == ci).astype(F32)
    gall = g_ref[...]
    Gc = _dot_hi(incl.astype(F32), gall)
    beta = beta_ref[...]
    gain = gn_ref[...]

    n_sq = int(round(math.log2(C))) - 1
    heads = range(hb)
    kn, qn, kk, qk = [], [], [], []
    for jk in range(hb // rep):
        kf = k_ref[:, jk * dk:(jk + 1) * dk].astype(F32)
        qf = q_ref[:, jk * dk:(jk + 1) * dk].astype(F32)
        kn.append(kf * lax.rsqrt(jnp.sum(kf * kf, axis=-1, keepdims=True) + L2_EPS))
        qn.append(qf * lax.rsqrt(jnp.sum(qf * qf, axis=-1, keepdims=True) + L2_EPS) * (dk ** -0.5))
        kb = kn[jk].astype(BF16)
        kk.append(_dot_nt(kb, kb))
        qk.append(_dot_nt(qn[jk].astype(BF16), kb))
    Gcol = [Gc[:, j:j + 1] for j in heads]
    bcol = [beta[:, j:j + 1] for j in heads]
    dec_incl, Nm = [], []
    for j in heads:
        Grow = jnp.sum(eye * Gcol[j], axis=0, keepdims=True)
        diff = Gcol[j] - Grow
        dec = jnp.where(incl, jnp.exp(jnp.where(incl, diff, 0.0)), 0.0)
        dec_incl.append(dec)
        Nm.append(-(bcol[j] * kk[j // rep] * jnp.where(strict, dec, 0.0)))
    Tm = [eye + Nm[j] for j in heads]
    Nlr = [_x4_both(Nm[j]) for j in heads]
    for lvl in range(n_sq):
        Np = [_dot_x4(*Nlr[j]) for j in heads]
        Nlr = [_x4_both(Np[j]) if lvl + 1 < n_sq else (None, _x4_rhs(Np[j])) for j in heads]
        Tm = [Tm[j] + _dot_x4(_x4_lhs(Tm[j]), Nlr[j][1]) for j in heads]
    eG = [jnp.exp(Gcol[j]) for j in heads]
    sol = []
    for j in heads:
        vf = v_ref[:, j * dv:(j + 1) * dv].astype(F32)
        rhs = jnp.concatenate([bcol[j] * vf, (bcol[j] * eG[j]) * kn[j // rep]], axis=1)
        sol.append(_dot_x4(_x4_lhs(Tm[j]), _x4_rhs(rhs)))
    Glast = [Gc[C - 1:C, j:j + 1] for j in heads]
    Sold = [S[j] for j in heads]
    Sb = [Sold[j].astype(BF16) for j in heads]
    wq, Pb, kd = [], [], []
    for j in heads:
        q_g = qn[j // rep] * eG[j]
        wq.append(jnp.concatenate([sol[j][:, dv:].astype(BF16), q_g.astype(BF16)], axis=0))
        Pb.append((qk[j // rep] * dec_incl[j]).astype(BF16))
        kd.append((kn[j // rep] * jnp.exp(Glast[j] - Gcol[j])).astype(BF16))
    wqS = [_dot(wq[j], Sb[j]) for j in heads]
    ub = [(sol[j][:, :dv] - wqS[j][:C]).astype(BF16) for j in heads]
    for j in heads:
        S[j] = jnp.exp(Glast[j]) * Sold[j] + _dot_tn(kd[j], ub[j])
    for j in heads:
        o = wqS[j][C:] + _dot(Pb[j], ub[j])
        on = o * lax.rsqrt(jnp.mean(o * o, axis=-1, keepdims=True) + NORM_EPS) * gain
        o_ref[:, j * dv:(j + 1) * dv] = (on * z_ref[:, j * dv:(j + 1) * dv].astype(F32)).astype(o_ref.dtype)

    @pl.when(n == n_chunks - 1)
    def _():
        sout_ref[0] = S[...]


def _gdn_core(qkv, sz, beta, g, S0, out_norm, *, nseq, chunk, hb):
    T = qkv.shape[0]
    _, HV, dk, dv = S0.shape
    VW = HV * dv
    QK = (qkv.shape[1] - VW) // 2
    HK = QK // dk
    rep = HV // HK
    HG = HV // hb
    kb = hb // rep
    L = T // nseq
    n_chunks = L // chunk

    def group_major(t):
        t = t.reshape(T, HG, hb).transpose(1, 0, 2)
        return jnp.pad(t, ((0, 0), (0, 0), (0, LANES - hb)))

    rowblk = lambda b, h, n: b * n_chunks + n
    in_specs = [
        pl.BlockSpec((chunk, kb * dk), lambda b, h, n: (rowblk(b, h, n), h)),
        pl.BlockSpec((chunk, kb * dk), lambda b, h, n: (rowblk(b, h, n), HG + h)),
        pl.BlockSpec((chunk, hb * dv), lambda b, h, n: (rowblk(b, h, n), 2 * QK // (hb * dv) + h)),
        pl.BlockSpec((chunk, hb * dv), lambda b, h, n: (rowblk(b, h, n), h)),
        pl.BlockSpec((None, chunk, LANES), lambda b, h, n: (h, rowblk(b, h, n), 0)),
        pl.BlockSpec((None, chunk, LANES), lambda b, h, n: (h, rowblk(b, h, n), 0)),
        pl.BlockSpec((1, hb, dk, dv), lambda b, h, n: (b, h, 0, 0)),
        pl.BlockSpec((1, dv), lambda b, h, n: (0, 0)),
    ]
    out_specs = [
        pl.BlockSpec((chunk, hb * dv), lambda b, h, n: (rowblk(b, h, n), h)),
        pl.BlockSpec((1, hb, dk, dv), lambda b, h, n: (b, h, 0, 0)),
    ]
    o, S = pl.pallas_call(
        functools.partial(_gdn_body, hb=hb, rep=rep, dk=dk, dv=dv, n_chunks=n_chunks),
        grid=(nseq, HG, n_chunks),
        in_specs=in_specs,
        out_specs=out_specs,
        out_shape=[jax.ShapeDtypeStruct((T, VW), BF16), jax.ShapeDtypeStruct(S0.shape, F32)],
        scratch_shapes=[pltpu.VMEM((hb, dk, dv), F32)],
        compiler_params=_cparams(("parallel", "parallel", "arbitrary")),
        name="gdn_core",
    )(qkv, qkv, qkv, sz, group_major(beta), group_major(g), S0.astype(F32),
      out_norm.astype(F32).reshape(1, dv))
    return o, S


def _mmres_body(x_ref, w_ref, res_ref, *refs, n_norm, with_h):
    gains = refs[:n_norm]
    outs = refs[n_norm:]
    h = res_ref[...] + _dot(x_ref[...], w_ref[...])
    k = 0
    if with_h:
        outs[0][...] = h
        k = 1
    if n_norm:
        inv = lax.rsqrt(jnp.mean(h * h, axis=-1, keepdims=True) + NORM_EPS)
        y = h * inv
        for i in range(n_norm):
            outs[k + i][...] = (y * gains[i][...]).astype(outs[k + i].dtype)


def _matmul_residual(x, w, res, gains, norm_dtype, with_h, name):
    T, K = x.shape
    D = w.shape[1]
    tm = _tile(T, 256, BF16_SUBLANES)
    row = lambda i: (i, 0)
    fixed = lambda i: (0, 0)
    n_norm = len(gains)
    in_specs = [pl.BlockSpec((tm, K), row),
                pl.BlockSpec((K, D), fixed, pipeline_mode=pl.Buffered(1)),
                pl.BlockSpec((tm, D), row)]
    in_specs += [pl.BlockSpec((1, D), fixed) for _ in gains]
    out_shape = ([jax.ShapeDtypeStruct((T, D), F32)] if with_h else []) \
        + [jax.ShapeDtypeStruct((T, D), norm_dtype) for _ in gains]
    outs = pl.pallas_call(
        functools.partial(_mmres_body, n_norm=n_norm, with_h=with_h),
        grid=(T // tm,),
        in_specs=in_specs,
        out_specs=[pl.BlockSpec((tm, D), row) for _ in out_shape],
        out_shape=out_shape,
        compiler_params=_cparams(("parallel",)),
        name=name,
    )(x, w, res, *[g.astype(F32).reshape(1, D) for g in gains])
    return outs


def _bucket_table(num_buckets):
    half = num_buckets // 2
    exact = half // 2
    n = np.arange(MAX_DISTANCE + 1)
    large = exact + (np.log(np.maximum(n, 1).astype(np.float32) / np.float32(exact))
                     / np.float32(math.log(MAX_DISTANCE / exact)) * np.float32(half - exact)).astype(np.int32)
    large = np.minimum(large, half - 1)
    return np.where(n < exact, n, large), half


def _bias_body(rb_ref, out_ref, *, tiles, steps, half):
    h = pl.program_id(0)
    for t, (q0, k0) in enumerate(tiles):
        nq, nk = out_ref.shape[2], out_ref.shape[3]
        qpos = q0 + lax.broadcasted_iota(jnp.int32, (nq, nk), 0)
        kpos = k0 + lax.broadcasted_iota(jnp.int32, (nq, nk), 1)
        rel = kpos - qpos
        n = jnp.abs(rel)
        neg = jnp.full((nq, nk), rb_ref[0, h], F32)
        pos = jnp.full((nq, nk), rb_ref[half, h], F32)
        for thr, b in steps:
            ge = n >= thr
            neg = jnp.where(ge, rb_ref[b, h], neg)
            pos = jnp.where(ge, rb_ref[half + b, h], pos)
        bias = jnp.where(rel > 0, pos, neg)
        visible = kpos // CHUNK <= qpos // CHUNK
        out_ref[t, 0] = jnp.where(visible, bias, NEG_INF)


def _bias_tiles(rel_bias, tiles, nq, nk):
    NB, H = rel_bias.shape
    table, half = _bucket_table(NB)
    steps = [(int(i), int(table[i])) for i in range(1, len(table)) if table[i] != table[i - 1]]
    return pl.pallas_call(
        functools.partial(_bias_body, tiles=tuple(tiles), steps=tuple(steps), half=half),
        grid=(H,),
        in_specs=[pl.BlockSpec(memory_space=pltpu.SMEM)],
        out_specs=pl.BlockSpec((len(tiles), 1, nq, nk), lambda h: (0, h, 0, 0)),
        out_shape=jax.ShapeDtypeStruct((len(tiles), H, nq, nk), F32),
        compiler_params=_cparams(("parallel",)),
        name="t5_bias_tiles",
    )(rel_bias.astype(F32))


def _lane_tile(x, n):
    if n <= LANES:
        return x[:, :n]
    return jnp.concatenate([x] * (n // LANES), axis=1)


def _softmax_update(s, vb, m_ref, l_ref, acc_ref, idx):
    m_prev = m_ref[idx]
    l_prev = l_ref[idx]
    m_new = jnp.maximum(m_prev, jnp.max(s, axis=1, keepdims=True))
    alpha = jnp.exp(m_prev - m_new)
    p = jnp.exp(s - _lane_tile(m_new, s.shape[1]))
    l_ref[idx] = alpha * l_prev + jnp.sum(p, axis=1, keepdims=True)
    acc_ref[idx] = acc_ref[idx] * _lane_tile(alpha, vb.shape[1]) + _dot(p.astype(BF16), vb)
    m_ref[idx] = m_new


def _diff_finalize(lam_ref, sn_ref, o_ref, l_ref, acc_ref, *, H, d, lam_init):
    lv = lam_ref[...]
    lam = (jnp.exp(jnp.sum(lv[0:1] * lv[1:2], axis=1, keepdims=True))
           - jnp.exp(jnp.sum(lv[2:3] * lv[3:4], axis=1, keepdims=True)) + lam_init)
    for h in range(H):
        o = (acc_ref[2 * h] / _lane_tile(l_ref[2 * h], 2 * d)
             - lam * (acc_ref[2 * h + 1] / _lane_tile(l_ref[2 * h + 1], 2 * d)))
        on = o * lax.rsqrt(jnp.mean(o * o, axis=-1, keepdims=True) + DIFF_SUBLN_EPS) * sn_ref[...]
        o_ref[:, h * 2 * d:(h + 1) * 2 * d] = (on * (1.0 - lam_init)).astype(o_ref.dtype)


def _attn_body(qi_ref, kj_ref, q_ref, k_ref, v_ref, bias_ref, lam_ref, sn_ref, o_ref, m_ref, l_ref, acc_ref,
               *, H, d, lam_init, kv_mult, n_tiles):
    p = pl.program_id(1)
    qi = qi_ref[p]
    kj = kj_ref[p]

    @pl.when(kj == 0)
    def _():
        m_ref[...] = jnp.full(m_ref.shape, -jnp.inf, F32)
        l_ref[...] = jnp.zeros(l_ref.shape, F32)
        acc_ref[...] = jnp.zeros(acc_ref.shape, F32)

    bidx = jnp.minimum(qi - kv_mult * kj, n_tiles - 1)
    for h in range(H):
        vb = v_ref[:, h * 2 * d:(h + 1) * 2 * d]
        for mp in range(2):
            c0 = (2 * h + mp) * d
            s = _dot_nt(q_ref[:, c0:c0 + d], k_ref[:, c0:c0 + d]) + bias_ref[bidx, h]
            _softmax_update(s, vb, m_ref, l_ref, acc_ref, 2 * h + mp)

    @pl.when(kj == qi // kv_mult)
    def _():
        _diff_finalize(lam_ref, sn_ref, o_ref, l_ref, acc_ref, H=H, d=d, lam_init=lam_init)


def _diff_attention_prompt(q, k, v, rel_bias, lam_vecs, sub_norm, lam_init, *, nseq, d):
    T, QW = q.shape
    H = QW // (2 * d)
    L = T // nseq
    tq = _tile(L, 256, LANES)
    kv_mult = 2 if L % (2 * tq) == 0 else 1
    tk = kv_mult * tq
    assert tq % CHUNK == 0 and L % tq == 0
    nq, nk = L // tq, L // tk
    pairs = [(i, j) for i in range(nq) for j in range(i // kv_mult + 1)]
    qi = jnp.asarray([pq for pq, _ in pairs], jnp.int32)
    kj = jnp.asarray([pk for _, pk in pairs], jnp.int32)
    n_tiles = -(-(tk + MAX_DISTANCE - 1) // tq) + 1
    bias = _bias_tiles(rel_bias, [(t * tq, 0) for t in range(n_tiles)], tq, tk)
    qrow = lambda b, p, qi_r, kj_r: (b * nq + qi_r[p], 0)
    krow = lambda b, p, qi_r, kj_r: (b * nk + kj_r[p], 0)
    fixed2 = lambda b, p, qi_r, kj_r: (0, 0)
    grid_spec = pltpu.PrefetchScalarGridSpec(
        num_scalar_prefetch=2,
        grid=(nseq, len(pairs)),
        in_specs=[
            pl.BlockSpec((tq, QW), qrow),
            pl.BlockSpec((tk, QW), krow),
            pl.BlockSpec((tk, QW), krow),
            pl.BlockSpec((n_tiles, H, tq, tk), lambda b, p, qi_r, kj_r: (0, 0, 0, 0),
                         pipeline_mode=pl.Buffered(1)),
            pl.BlockSpec((4, d), fixed2),
            pl.BlockSpec((1, 2 * d), fixed2),
        ],
        out_specs=pl.BlockSpec((tq, QW), qrow),
        scratch_shapes=[pltpu.VMEM((2 * H, tq, LANES), F32), pltpu.VMEM((2 * H, tq, LANES), F32),
                        pltpu.VMEM((2 * H, tq, 2 * d), F32)],
    )
    return pl.pallas_call(
        functools.partial(_attn_body, H=H, d=d, lam_init=lam_init, kv_mult=kv_mult, n_tiles=n_tiles),
        grid_spec=grid_spec,
        out_shape=jax.ShapeDtypeStruct((T, QW), BF16),
        compiler_params=_cparams(("parallel", "arbitrary")),
        name="diff_attention_prompt",
    )(qi, kj, q, k, v, bias, lam_vecs, sub_norm.astype(F32).reshape(1, 2 * d))


def _attn_dec_body(q_ref, ck_ref, cv_ref, kn_ref, vn_ref, bc_ref, bn_ref, lam_ref, sn_ref, o_ref,
                   m_ref, l_ref, acc_ref, *, H, d, lam_init, n_blocks):
    kb_i = pl.program_id(1)

    @pl.when(kb_i == 0)
    def _():
        m_ref[...] = jnp.full(m_ref.shape, -jnp.inf, F32)
        l_ref[...] = jnp.zeros(l_ref.shape, F32)
        acc_ref[...] = jnp.zeros(acc_ref.shape, F32)

    def sweep(k_all, v_all, b_ref):
        for h in range(H):
            vb = v_all[:, h * 2 * d:(h + 1) * 2 * d]
            for mp in range(2):
                c0 = (2 * h + mp) * d
                s = _dot_nt(q_ref[:, c0:c0 + d], k_all[:, c0:c0 + d]) + b_ref[h]
                _softmax_update(s, vb, m_ref, l_ref, acc_ref, 2 * h + mp)

    sweep(ck_ref[0].astype(BF16), cv_ref[0].astype(BF16), bc_ref)

    @pl.when(kb_i == n_blocks - 1)
    def _():
        sweep(kn_ref[...], vn_ref[...], bn_ref)
        _diff_finalize(lam_ref, sn_ref, o_ref, l_ref, acc_ref, H=H, d=d, lam_init=lam_init)


def _diff_attention_decode(q, k_new, v_new, cache_k, cache_v, rel_bias, lam_vecs, sub_norm, lam_init, *, d):
    B, P, QW = cache_k.shape
    H = QW // (2 * d)
    Lq = q.shape[0] // B
    tk = _tile(P, 512, LANES)
    nb = P // tk
    bias_c = _bias_tiles(rel_bias, [(P, j * tk) for j in range(nb)], Lq, tk)
    bias_n = _bias_tiles(rel_bias, [(P, P)], Lq, Lq)[0]
    seq = lambda b, j: (b, 0)
    fixed2 = lambda b, j: (0, 0)
    return pl.pallas_call(
        functools.partial(_attn_dec_body, H=H, d=d, lam_init=lam_init, n_blocks=nb),
        grid=(B, nb),
        in_specs=[
            pl.BlockSpec((Lq, QW), seq),
            pl.BlockSpec((1, tk, QW), lambda b, j: (b, j, 0)),
            pl.BlockSpec((1, tk, QW), lambda b, j: (b, j, 0)),
            pl.BlockSpec((Lq, QW), seq),
            pl.BlockSpec((Lq, QW), seq),
            pl.BlockSpec((None, H, Lq, tk), lambda b, j: (j, 0, 0, 0)),
            pl.BlockSpec((H, Lq, Lq), lambda b, j: (0, 0, 0)),
            pl.BlockSpec((4, d), fixed2),
            pl.BlockSpec((1, 2 * d), fixed2),
        ],
        out_specs=pl.BlockSpec((Lq, QW), seq),
        out_shape=jax.ShapeDtypeStruct((B * Lq, QW), BF16),
        scratch_shapes=[pltpu.VMEM((2 * H, Lq, LANES), F32), pltpu.VMEM((2 * H, Lq, LANES), F32),
                        pltpu.VMEM((2 * H, Lq, 2 * d), F32)],
        compiler_params=_cparams(("parallel", "arbitrary")),
        name="diff_attention_decode",
    )(q, cache_k, cache_v, k_new, v_new, bias_c, bias_n, lam_vecs, sub_norm.astype(F32).reshape(1, 2 * d))


def _trunk(x, p, st_gdn, st_gdn_conv, st_ffn_conv, cache_k, cache_v):
    B, L, D = x.shape
    fresh = cache_k is None
    T = B * L
    depth = p['f_norm'].shape[0]
    n_a = p['a_norm'].shape[0]
    HV, dv = p['a_log'].shape[1], p['a_out_norm'].shape[1]
    VW = HV * dv
    QKVW = p['a_w_conv'].shape[2]
    dk = p['gdn_head_k']
    d = p['b_lam_q1'].shape[1]
    DFF = p['f_w_down'].shape[1]
    QW = p['b_w_q'].shape[2]

    time_major = not fresh
    if time_major:
        h = x.transpose(1, 0, 2).reshape(T, D)
        stride, nseq_conv = B, 1
    else:
        h = x.reshape(T, D)
        stride, nseq_conv = 1, B

    def to_seq(t):
        return t.reshape(L, B, -1).transpose(1, 0, 2).reshape(T, -1) if time_major else t

    def to_time(t):
        return t.reshape(B, L, -1).transpose(1, 0, 2).reshape(T, -1) if time_major else t

    def conv_state_in(st, width, chans):
        if st is None:
            return jnp.zeros((nseq_conv, (width - 1) * stride, chans), F32)
        return st.transpose(1, 0, 2).reshape(1, (width - 1) * B, chans)

    def conv_state_out(st, width):
        if time_major:
            return st.reshape(width - 1, B, -1).transpose(1, 0, 2)
        return st

    new_S, new_gconv, new_fconv = [], [], []
    k_f32 = v_f32 = k_bf = v_bf = None
    xn = _rmsnorm_cast(h, p['a_norm'][0]) if n_a > 0 else _rmsnorm_cast(h, p['b_norm'][0])
    xn_kv = None
    y = None
    for layer in range(depth):
        if layer < n_a:
            i = layer
            w_in = p['a_w_in_bf'][i]
            w_ba = p['a_w_in'][i][:, QKVW + VW:]
            gw = p['a_w_conv'].shape[1]
            cst = conv_state_in(None if fresh else st_gdn_conv[i], gw, QKVW)
            qkv, gconv = _conv_matmul(xn, w_in, p['a_w_conv'][i], None, cst, groups=1, width=QKVW,
                                      stride=stride, nseq=nseq_conv, glu=False, name="gdn_qkv_conv",
                                      tm_pref=512, tn_pref=2048)
            (sz,) = _matmul(xn, w_in, lambda acc: (_silu(acc),), [BF16], "gdn_z_proj", col0=QKVW, width=VW)
            beta, g = _gate_proj(xn, w_ba[:, :HV], w_ba[:, HV:], p['a_log'][i], p['a_dt_bias'][i])
            S0 = jnp.zeros((B, HV, dk, dv), F32) if fresh else st_gdn[i]
            og, S = _gdn_core(to_seq(qkv), to_seq(sz), to_seq(beta), to_seq(g), S0, p['a_out_norm'][i],
                              nseq=B, chunk=min(CHUNK, L), hb=min(GDN_HEADS_PER_STEP, HV))
            new_S.append(S)
            new_gconv.append(conv_state_out(gconv, gw))
            h, xn = _matmul_residual(to_time(og), p['a_w_out_bf'][i], h, [p['f_norm'][layer]],
                                     BF16, True, "gdn_out_proj")
        else:
            j = layer - n_a
            lam_init = 0.8 - 0.6 * math.exp(-0.3 * layer)
            scale = d ** -0.5
            (q,) = _matmul(xn, p['b_w_q_bf'][j], lambda acc: (acc * scale,), [BF16], "diff_q_proj")
            lam_vecs = jnp.stack([p['b_lam_q1'][j], p['b_lam_k1'][j], p['b_lam_q2'][j],
                                  p['b_lam_k2'][j]]).astype(F32)
            if fresh:
                ao = _diff_attention_prompt(q, k_bf, v_bf, p['rel_bias'], lam_vecs, p['b_sub_norm'][j],
                                            lam_init, nseq=B, d=d)
            else:
                ao = _diff_attention_decode(to_seq(q), to_seq(k_bf), to_seq(v_bf),
                                            cache_k.reshape(B, -1, QW), cache_v.reshape(B, -1, QW),
                                            p['rel_bias'], lam_vecs, p['b_sub_norm'][j], lam_init, d=d)
                ao = to_time(ao)
            h, xn = _matmul_residual(ao, p['b_w_o_bf'][j], h, [p['f_norm'][layer]],
                                     BF16, True, "diff_out_proj")
        fw = p['f_w_conv'].shape[1]
        fst = conv_state_in(None if fresh else st_ffn_conv[layer], fw, 2 * DFF)
        act, fconv = _conv_matmul(xn, p['f_w_up_bf'][layer], p['f_w_conv'][layer],
                                  p['f_b_conv'][layer], fst, groups=2, width=2 * DFF, stride=stride,
                                  nseq=nseq_conv, glu=True, name="ffn_up_conv", tm_pref=512, tn_pref=2816)
        new_fconv.append(conv_state_out(fconv, fw))
        last = layer == depth - 1
        gains = []
        if layer == n_a - 1:
            gains.append(p['kv_norm'])
        if last:
            gains.append(p['final_norm'])
        elif layer + 1 < n_a:
            gains.append(p['a_norm'][layer + 1])
        else:
            gains.append(p['b_norm'][layer + 1 - n_a])
        outs = _matmul_residual(act, p['f_w_down_bf'][layer], h, gains,
                                F32 if last else BF16, not last, "ffn_down_proj")
        if last:
            if layer == n_a - 1:
                xn_kv = outs[0].astype(BF16)
            y = outs[-1]
        else:
            h = outs[0]
            if layer == n_a - 1:
                xn_kv = outs[1]
            xn = outs[-1]
        if layer == n_a - 1:
            k_f32, k_bf = _head_projection(xn_kv, p['w_kv_bf'], 0, QW, d, "k_proj")
            v_f32, v_bf = _head_projection(xn_kv, p['w_kv_bf'], QW, QW, 2 * d, "v_proj")

    n_kh = 2 * (QW // (2 * d))
    y = to_seq(y).reshape(B, L, D)
    k_sh = to_seq(k_f32).reshape(B, L, n_kh, d)
    v_sh = to_seq(v_f32).reshape(B, L, n_kh // 2, 2 * d)
    return y, jnp.stack(new_S), jnp.stack(new_gconv), jnp.stack(new_fconv), k_sh, v_sh


def kernel(x_prompt, x_sample, state_gdn, state_gdn_conv, state_ffn_conv, cache_k, cache_v, a_norm, a_w_in, a_w_conv, a_log, a_dt_bias, a_out_norm, a_w_out, kv_norm, w_kv, b_norm, b_w_q, b_lam_q1, b_lam_k1, b_lam_q2, b_lam_k2, b_sub_norm, b_w_o, rel_bias, f_norm, f_w_up, f_w_conv, f_b_conv, f_w_down, final_norm):
    p = {
        'a_norm': a_norm, 'a_w_in': a_w_in, 'a_w_conv': a_w_conv, 'a_log': a_log,
        'a_dt_bias': a_dt_bias, 'a_out_norm': a_out_norm, 'a_w_out': a_w_out,
        'kv_norm': kv_norm, 'w_kv': w_kv,
        'b_norm': b_norm, 'b_w_q': b_w_q, 'b_lam_q1': b_lam_q1, 'b_lam_k1': b_lam_k1,
        'b_lam_q2': b_lam_q2, 'b_lam_k2': b_lam_k2, 'b_sub_norm': b_sub_norm, 'b_w_o': b_w_o,
        'rel_bias': rel_bias,
        'f_norm': f_norm, 'f_w_up': f_w_up, 'f_w_conv': f_w_conv, 'f_b_conv': f_b_conv,
        'f_w_down': f_w_down, 'final_norm': final_norm,
        'gdn_head_k': state_gdn.shape[3],
    }
    for name in ('a_w_in', 'a_w_out', 'w_kv', 'b_w_q', 'b_w_o', 'f_w_up', 'f_w_down'):
        p[name + '_bf'] = p[name].astype(BF16)
    out_p = _trunk(x_prompt, p, None, None, None, None, None)
    out_s = _trunk(x_sample, p, state_gdn, state_gdn_conv, state_ffn_conv, cache_k, cache_v)
    return (out_p[0], out_s[0]) + tuple(out_p[1:]) + tuple(out_s[1:])
```

```python
import functools
import math

import numpy as np
import jax
import jax.numpy as jnp
from jax import lax
from jax.experimental import pallas as pl
from jax.experimental.pallas import tpu as pltpu

F32 = jnp.float32
BF16 = jnp.bfloat16

CHUNK = 64
NORM_EPS = 1e-6
DIFF_SUBLN_EPS = 1e-5
L2_EPS = 1e-6
MAX_DISTANCE = 128
NEG_INF = -1e30
LOG2_E = math.log2(math.e)
GDN_HEADS_PER_STEP = 16

V7X_VMEM_LIMIT = 56 * 1024 * 1024
RESIDUAL_MATMUL_VMEM_BUDGET = 46 * 1024 * 1024
LANES = 128
V7X_MXU_COLS = 256
F32_SUBLANES = 8
BF16_SUBLANES = 16


def _cparams(sem):
    return pltpu.CompilerParams(dimension_semantics=sem, vmem_limit_bytes=V7X_VMEM_LIMIT)


def _tile(n, pref, mult):
    if n <= pref:
        return n
    t = (pref // mult) * mult
    while t >= mult:
        if n % t == 0:
            return t
        t -= mult
    return n


def _wspec(w, block, index_map, **kw):
    if isinstance(w, tuple):
        layer = w[0]
        return pl.BlockSpec((None,) + block, lambda *a: (layer,) + index_map(*a), **kw)
    return pl.BlockSpec(block, index_map, **kw)


def _warr(w):
    return w[1] if isinstance(w, tuple) else w


def _dot(a, b):
    return jnp.dot(a, b, preferred_element_type=F32)


def _dot_nt(a, b):
    return lax.dot_general(a, b, (((1,), (1,)), ((), ())), preferred_element_type=F32)


def _dot_tn(a, b):
    return lax.dot_general(a, b, (((0,), (0,)), ((), ())), preferred_element_type=F32)


def _dot_hi(a, b):
    return jnp.dot(a, b, preferred_element_type=F32, precision=lax.Precision.HIGHEST)


def _split_bf16(x):
    bits = lax.bitcast_convert_type(x, jnp.uint32) & jnp.uint32(0xFFFF0000)
    hi = lax.bitcast_convert_type(bits, F32)
    return hi, x - hi


def _x4_lhs(x, parts=None):
    xh, xl = parts or _split_bf16(x)
    xx = jnp.concatenate([xh, xl], axis=1).astype(BF16)
    return jnp.concatenate([xx, xx], axis=1)


def _x4_rhs(y, parts=None):
    yh, yl = parts or _split_bf16(y)
    return jnp.concatenate([yh, yh, yl, yl], axis=0).astype(BF16)


def _x4_both(x):
    parts = _split_bf16(x)
    return _x4_lhs(x, parts), _x4_rhs(x, parts)


def _dot_x4(lhs4, rhs4):
    return _dot(lhs4, rhs4)


def _silu(x):
    h = 0.5 * x
    return h + h * jnp.tanh(h)


def _rms_body(x_ref, g_ref, o_ref):
    x = x_ref[...]
    ms = jnp.mean(x * x, axis=-1, keepdims=True)
    o_ref[...] = (x * lax.rsqrt(ms + NORM_EPS) * g_ref[...]).astype(o_ref.dtype)


def _rmsnorm_cast(x, g):
    T, D = x.shape
    tm = _tile(T, 512, BF16_SUBLANES)
    return pl.pallas_call(
        _rms_body,
        grid=(T // tm,),
        in_specs=[pl.BlockSpec((tm, D), lambda i: (i, 0)), pl.BlockSpec((1, D), lambda i: (0, 0))],
        out_specs=pl.BlockSpec((tm, D), lambda i: (i, 0)),
        out_shape=jax.ShapeDtypeStruct((T, D), BF16),
        compiler_params=_cparams(("parallel",)),
        name="rmsnorm_cast",
    )(x, g.reshape(1, D).astype(F32))


def _mm_body(x_ref, w_ref, *o_refs, epilogue):
    acc = _dot(x_ref[...], w_ref[...])
    for o_ref, val in zip(o_refs, epilogue(acc)):
        o_ref[...] = val.astype(o_ref.dtype)


def _matmul(x, w, epilogue, out_dtypes, name, col0=0, width=None, tm_pref=1024, tn_pref=1024):
    T, K = x.shape
    N = _warr(w).shape[-1] if width is None else width
    tm = _tile(T, tm_pref, BF16_SUBLANES)
    tn = _tile(math.gcd(N, col0) if col0 else N, tn_pref, LANES)
    cb = col0 // tn
    outs = pl.pallas_call(
        functools.partial(_mm_body, epilogue=epilogue),
        grid=(N // tn, T // tm),
        in_specs=[pl.BlockSpec((tm, K), lambda n, m: (m, 0)), _wspec(w, (K, tn), lambda n, m: (0, n + cb))],
        out_specs=[pl.BlockSpec((tm, tn), lambda n, m: (m, n)) for _ in out_dtypes],
        out_shape=[jax.ShapeDtypeStruct((T, N), dt) for dt in out_dtypes],
        compiler_params=_cparams(("parallel", "parallel")),
        name=name,
    )(x, _warr(w))
    return outs


def _headproj_body(x_ref, w_ref, o3_ref, obf_ref, *, hd):
    acc = _dot(x_ref[...], w_ref[...])
    obf_ref[...] = acc.astype(obf_ref.dtype)
    for hh in range(o3_ref.shape[1]):
        o3_ref[:, hh, :] = acc[:, hh * hd:(hh + 1) * hd]


def _head_projection(x, w, col0, width, hd, name):
    T, K = x.shape
    nh = width // hd
    hb = nh if nh <= F32_SUBLANES else F32_SUBLANES
    tn = hb * hd
    tm = _tile(T, 512, BF16_SUBLANES)
    cb = col0 // tn
    return pl.pallas_call(
        functools.partial(_headproj_body, hd=hd),
        grid=(width // tn, T // tm),
        in_specs=[pl.BlockSpec((tm, K), lambda n, m: (m, 0)), pl.BlockSpec((K, tn), lambda n, m: (0, n + cb))],
        out_specs=[pl.BlockSpec((tm, hb, hd), lambda n, m: (m, n, 0)), pl.BlockSpec((tm, tn), lambda n, m: (m, n))],
        out_shape=[jax.ShapeDtypeStruct((T, nh, hd), F32), jax.ShapeDtypeStruct((T, width), BF16)],
        compiler_params=_cparams(("parallel", "parallel")),
        name=name,
    )(x, w)


def _gate_body(x_ref, wb_ref, wa_ref, alog_ref, dtb_ref, beta_ref, gcum_ref, *, stride, chunk):
    x = x_ref[...]
    b = _dot(x, wb_ref[...])
    a = _dot(x, wa_ref[...]) + dtb_ref[...]
    beta_ref[...] = 1.0 / (1.0 + jnp.exp(-b))
    softplus = jnp.maximum(a, 0.0) + jnp.log(1.0 + jnp.exp(-jnp.abs(a)))
    g = -jnp.exp(alog_ref[...]) * softplus
    bs = stride * chunk
    ri = lax.broadcasted_iota(jnp.int32, (bs, bs), 0)
    ci = lax.broadcasted_iota(jnp.int32, (bs, bs), 1)
    earlier = jnp.logical_and(ci <= ri, (ri - ci) % stride == 0).astype(F32)
    for r0 in range(0, x.shape[0], bs):
        gcum_ref[r0:r0 + bs, :] = _dot_hi(earlier, g[r0:r0 + bs])


def _gate_proj(xn, w_b, w_a, a_log, dt_bias, *, stride, chunk):
    T, K = xn.shape
    H = w_b.shape[1]
    pad = LANES - H
    wb = jnp.pad(w_b, ((0, 0), (0, pad))).astype(BF16)
    wa = jnp.pad(w_a, ((0, 0), (0, pad))).astype(BF16)
    al = jnp.pad(a_log.astype(F32), (0, pad)).reshape(1, LANES)
    db = jnp.pad(dt_bias.astype(F32), (0, pad)).reshape(1, LANES)
    tm = _tile(T, 1024, stride * chunk)
    row = pl.BlockSpec((tm, K), lambda i: (i, 0))
    wsp = pl.BlockSpec((K, LANES), lambda i: (0, 0))
    vsp = pl.BlockSpec((1, LANES), lambda i: (0, 0))
    osp = pl.BlockSpec((tm, LANES), lambda i: (i, 0))
    beta, gcum = pl.pallas_call(
        functools.partial(_gate_body, stride=stride, chunk=chunk),
        grid=(T // tm,),
        in_specs=[row, wsp, wsp, vsp, vsp],
        out_specs=[osp, osp],
        out_shape=[jax.ShapeDtypeStruct((T, LANES), F32)] * 2,
        compiler_params=_cparams(("parallel",)),
        name="gdn_gate_proj",
    )(xn, wb, wa, al, db)
    return beta[:, :H], gcum[:, :H]


def _convmm_body(*refs, G, W, stride, tm, pad, tps, glu, n_sub):
    hist = (W - 1) * stride
    it = iter(refs)
    x_ref = next(it)
    w_refs = [next(it) for _ in range(G)]
    wc_refs = [next(it) for _ in range(G)]
    b_refs = [next(it) for _ in range(G)] if glu else None
    st_refs = [next(it) for _ in range(G)]
    out_ref = next(it)
    nst_refs = [next(it) for _ in range(G)]
    yscs = [[next(it) for _ in range(n_sub)] for _ in range(G)]

    m = pl.program_id(1)
    first = (m % tps) == 0
    last = (m % tps) == tps - 1
    sub = out_ref.shape[1] // n_sub

    @pl.when(first)
    def _():
        for g in range(G):
            for c in range(n_sub):
                yscs[g][c][pad - hist:pad, :] = st_refs[g][:, c * sub:(c + 1) * sub]

    @pl.when(jnp.logical_not(first))
    def _():
        for g in range(G):
            for c in range(n_sub):
                yscs[g][c][0:pad, :] = yscs[g][c][tm:tm + pad, :]

    def project(c):
        cols = slice(c * sub, (c + 1) * sub)
        for g in range(G):
            yscs[g][c][pad:pad + tm, :] = _dot(x_ref[...], w_refs[g][:, cols])

    def conv_act(c):
        cols = slice(c * sub, (c + 1) * sub)
        convs = []
        for g in range(G):
            acc = None
            for i in range(W):
                off = pad - (W - 1 - i) * stride
                term = yscs[g][c][off:off + tm, :] * wc_refs[g][i:i + 1, cols]
                acc = term if acc is None else acc + term
            if glu:
                acc = acc + b_refs[g][:, cols]
            convs.append(acc)
        res = _silu(convs[0]) * convs[1] if glu else _silu(convs[0])
        out_ref[:, cols] = res.astype(out_ref.dtype)

    project(0)
    for c in range(1, n_sub):
        project(c)
        conv_act(c - 1)
    conv_act(n_sub - 1)

    @pl.when(last)
    def _():
        for g in range(G):
            for c in range(n_sub):
                nst_refs[g][:, c * sub:(c + 1) * sub] = yscs[g][c][pad + tm - hist:pad + tm, :]


def _conv_matmul(x, w, wc, bias, state, *, groups, width, stride, nseq, glu, name, tm_pref, tn_pref):
    T, K = x.shape
    W = wc.shape[0]
    Ng = width // groups
    hist = (W - 1) * stride
    rows = T // nseq
    tm = _tile(rows, tm_pref, BF16_SUBLANES)
    tn = _tile(Ng, tn_pref, V7X_MXU_COLS)
    n_sub = max(tn // V7X_MXU_COLS, 1)
    tps = rows // tm
    pad = -(-hist // F32_SUBLANES) * F32_SUBLANES
    assert tm >= pad and state.shape == (nseq, hist, groups * Ng)
    nb = Ng // tn

    def col(g):
        return lambda n, m: (0, n + g * nb)

    def stcol(g):
        return lambda n, m: (m // tps, 0, n + g * nb)

    in_specs = [pl.BlockSpec((tm, K), lambda n, m: (m, 0))]
    args = [x]
    in_specs += [_wspec(w, (K, tn), col(g), pipeline_mode=pl.Buffered(1)) for g in range(groups)]
    args += [_warr(w)] * groups
    in_specs += [pl.BlockSpec((W, tn), col(g)) for g in range(groups)]
    args += [wc.astype(F32)] * groups
    if glu:
        in_specs += [pl.BlockSpec((1, tn), col(g)) for g in range(groups)]
        args += [bias.astype(F32).reshape(1, -1)] * groups
    in_specs += [pl.BlockSpec((None, hist, tn), stcol(g)) for g in range(groups)]
    args += [state.astype(F32)] * groups
    out_specs = [pl.BlockSpec((tm, tn), lambda n, m: (m, n))]
    out_specs += [pl.BlockSpec((None, hist, tn), stcol(0)) for g in range(groups)]
    out_shape = [jax.ShapeDtypeStruct((T, Ng), BF16)]
    out_shape += [jax.ShapeDtypeStruct((nseq, hist, Ng), F32) for g in range(groups)]
    outs = pl.pallas_call(
        functools.partial(_convmm_body, G=groups, W=W, stride=stride, tm=tm, pad=pad, tps=tps, glu=glu,
                          n_sub=n_sub),
        grid=(nb, T // tm),
        in_specs=in_specs,
        out_specs=out_specs,
        out_shape=out_shape,
        scratch_shapes=[pltpu.VMEM((pad + tm, tn // n_sub), F32) for _ in range(groups * n_sub)],
        compiler_params=_cparams(("parallel", "arbitrary")),
        name=name,
    )(*args)
    new_state = outs[1] if groups == 1 else jnp.concatenate(outs[1:], axis=-1)
    return outs[0], new_state


def _gdn_body(q_ref, k_ref, v_ref, z_ref, beta_ref, g_ref, s0_ref, gn_ref, o_ref, sout_ref, S,
              *, hb, rep, dk, dv, n_chunks):
    n = pl.program_id(2)

    @pl.when(n == 0)
    def _():
        S[...] = s0_ref[0]

    C = q_ref.shape[0]
    ri = lax.broadcasted_iota(jnp.int32, (C, C), 0)
    ci = lax.broadcasted_iota(jnp.int32, (C, C), 1)
    incl = ri >= ci
    strict = ri > ci
    eye = (ri == ci).astype(F32)
    Gc = g_ref[...]
    beta = beta_ref[...]
    gain = gn_ref[...]

    n_sq = int(round(math.log2(C))) - 1
    heads = range(hb)
    kn, qn, kk, qk = [], [], [], []
    for jk in range(hb // rep):
        kf = k_ref[:, jk * dk:(jk + 1) * dk].astype(F32)
        qf = q_ref[:, jk * dk:(jk + 1) * dk].astype(F32)
        kn.append(kf * lax.rsqrt(jnp.sum(kf * kf, axis=-1, keepdims=True) + L2_EPS))
        qn.append(qf * lax.rsqrt(jnp.sum(qf * qf, axis=-1, keepdims=True) + L2_EPS) * (dk ** -0.5))
        kb = kn[jk].astype(BF16)
        kq = _dot_nt(jnp.concatenate([kb, qn[jk].astype(BF16)], axis=0), kb)
        kk.append(kq[:C])
        qk.append(kq[C:])
    Gcol = [Gc[:, j:j + 1] for j in heads]
    bcol = [beta[:, j:j + 1] for j in heads]
    dec_incl, Nm = [], []
    for j in heads:
        Grow = jnp.sum(eye * Gcol[j], axis=0, keepdims=True)
        diff = Gcol[j] - Grow
        dec = jnp.where(incl, jnp.exp(jnp.where(incl, diff, 0.0)), 0.0)
        dec_incl.append(dec)
        Nm.append(-(bcol[j] * kk[j // rep] * jnp.where(strict, dec, 0.0)))
    Tm = [eye + Nm[j] for j in heads]
    Np = [_dot_x4(*_x4_both(Nm[j])) for j in heads]
    for lvl in range(n_sq):
        nparts = [_split_bf16(Np[j]) for j in heads]
        rhs4 = [_x4_rhs(None, nparts[j]) for j in heads]
        tparts = [_split_bf16(Tm[j]) for j in heads]
        if lvl + 1 < n_sq:
            stacked = [tuple(jnp.concatenate([tp, npart], axis=0) for tp, npart in zip(tparts[j], nparts[j]))
                       for j in heads]
            res = [_dot_x4(_x4_lhs(None, stacked[j]), rhs4[j]) for j in heads]
            Tm = [Tm[j] + res[j][:C] for j in heads]
            Np = [res[j][C:] for j in heads]
        else:
            Tm = [Tm[j] + _dot_x4(_x4_lhs(None, tparts[j]), rhs4[j]) for j in heads]
    eG = [jnp.exp(Gcol[j]) for j in heads]
    sol = []
    for j in heads:
        vf = v_ref[:, j * dv:(j + 1) * dv].astype(F32)
        rhs = jnp.concatenate([bcol[j] * vf, (bcol[j] * eG[j]) * kn[j // rep]], axis=1)
        sol.append(_dot_x4(_x4_lhs(Tm[j]), _x4_rhs(rhs)))
    Glast = [Gc[C - 1:C, j:j + 1] for j in heads]
    Sold = [S[j] for j in heads]
    Sb = [Sold[j].astype(BF16) for j in heads]
    wq, pk = [], []
    for j in heads:
        q_g = qn[j // rep] * eG[j]
        wq.append(jnp.concatenate([sol[j][:, dv:].astype(BF16), q_g.astype(BF16)], axis=0))
        P = qk[j // rep] * dec_incl[j]
        k_d = kn[j // rep] * jnp.exp(Glast[j] - Gcol[j])
        pk.append(jnp.concatenate([P.astype(BF16), k_d.T.astype(BF16)], axis=0))
    wqS = [_dot(wq[j], Sb[j]) for j in heads]
    ub = [(sol[j][:, :dv] - wqS[j][:C]).astype(BF16) for j in heads]
    pku = [_dot(pk[j], ub[j]) for j in heads]
    for j in heads:
        S[j] = jnp.exp(Glast[j]) * Sold[j] + pku[j][C:]
    for j in heads:
        o = wqS[j][C:] + pku[j][:C]
        on = o * lax.rsqrt(jnp.mean(o * o, axis=-1, keepdims=True) + NORM_EPS) * gain
        o_ref[:, j * dv:(j + 1) * dv] = (on * z_ref[:, j * dv:(j + 1) * dv].astype(F32)).astype(o_ref.dtype)

    @pl.when(n == n_chunks - 1)
    def _():
        sout_ref[0] = S[...]


def _gdn_core(qkv, sz, beta, g, S0, out_norm, *, nseq, chunk, hb):
    T = qkv.shape[0]
    _, HV, dk, dv = S0.shape
    VW = HV * dv
    QK = (qkv.shape[1] - VW) // 2
    HK = QK // dk
    rep = HV // HK
    HG = HV // hb
    kb = hb // rep
    L = T // nseq
    n_chunks = L // chunk

    def group_major(t):
        t = t.reshape(T, HG, hb).transpose(1, 0, 2)
        return jnp.pad(t, ((0, 0), (0, 0), (0, LANES - hb)))

    rowblk = lambda b, h, n: b * n_chunks + n
    in_specs = [
        pl.BlockSpec((chunk, kb * dk), lambda b, h, n: (rowblk(b, h, n), h)),
        pl.BlockSpec((chunk, kb * dk), lambda b, h, n: (rowblk(b, h, n), HG + h)),
        pl.BlockSpec((chunk, hb * dv), lambda b, h, n: (rowblk(b, h, n), 2 * QK // (hb * dv) + h)),
        pl.BlockSpec((chunk, hb * dv), lambda b, h, n: (rowblk(b, h, n), h)),
        pl.BlockSpec((None, chunk, LANES), lambda b, h, n: (h, rowblk(b, h, n), 0)),
        pl.BlockSpec((None, chunk, LANES), lambda b, h, n: (h, rowblk(b, h, n), 0)),
        pl.BlockSpec((1, hb, dk, dv), lambda b, h, n: (b, h, 0, 0)),
        pl.BlockSpec((1, dv), lambda b, h, n: (0, 0)),
    ]
    out_specs = [
        pl.BlockSpec((chunk, hb * dv), lambda b, h, n: (rowblk(b, h, n), h)),
        pl.BlockSpec((1, hb, dk, dv), lambda b, h, n: (b, h, 0, 0)),
    ]
    o, S = pl.pallas_call(
        functools.partial(_gdn_body, hb=hb, rep=rep, dk=dk, dv=dv, n_chunks=n_chunks),
        grid=(nseq, HG, n_chunks),
        in_specs=in_specs,
        out_specs=out_specs,
        out_shape=[jax.ShapeDtypeStruct((T, VW), BF16), jax.ShapeDtypeStruct(S0.shape, F32)],
        scratch_shapes=[pltpu.VMEM((hb, dk, dv), F32)],
        compiler_params=_cparams(("parallel", "parallel", "arbitrary")),
        name="gdn_core",
    )(qkv, qkv, qkv, sz, group_major(beta), group_major(g), S0.astype(F32),
      out_norm.astype(F32).reshape(1, dv))
    return o, S


def _mmres_body(x_ref, w_ref, res_ref, *refs, n_norm, with_h):
    gains = refs[:n_norm]
    outs = refs[n_norm:]
    h = res_ref[...] + _dot(x_ref[...], w_ref[...])
    k = 0
    if with_h:
        outs[0][...] = h
        k = 1
    if n_norm:
        inv = lax.rsqrt(jnp.mean(h * h, axis=-1, keepdims=True) + NORM_EPS)
        y = h * inv
        for i in range(n_norm):
            outs[k + i][...] = (y * gains[i][...]).astype(outs[k + i].dtype)


def _matmul_residual(x, w, res, gains, norm_dtype, with_h, name):
    T, K = x.shape
    D = _warr(w).shape[-1]
    n_f32 = (1 if with_h else 0) + (len(gains) if norm_dtype == F32 else 0)
    n_bf16 = len(gains) if norm_dtype == BF16 else 0

    def vmem_bytes(rows):
        return K * D * 2 + 2 * rows * (K * 2 + D * 4 + n_f32 * D * 4 + n_bf16 * D * 2)

    tm = next(t for t in (1024, 512, 256, 128, BF16_SUBLANES)
              if t == BF16_SUBLANES or vmem_bytes(t) <= RESIDUAL_MATMUL_VMEM_BUDGET)
    tm = _tile(T, tm, BF16_SUBLANES)
    row = lambda i: (i, 0)
    fixed = lambda i: (0, 0)
    n_norm = len(gains)
    in_specs = [pl.BlockSpec((tm, K), row),
                _wspec(w, (K, D), fixed, pipeline_mode=pl.Buffered(1)),
                pl.BlockSpec((tm, D), row)]
    in_specs += [pl.BlockSpec((1, D), fixed) for _ in gains]
    out_shape = ([jax.ShapeDtypeStruct((T, D), F32)] if with_h else []) \
        + [jax.ShapeDtypeStruct((T, D), norm_dtype) for _ in gains]
    outs = pl.pallas_call(
        functools.partial(_mmres_body, n_norm=n_norm, with_h=with_h),
        grid=(T // tm,),
        in_specs=in_specs,
        out_specs=[pl.BlockSpec((tm, D), row) for _ in out_shape],
        out_shape=out_shape,
        compiler_params=_cparams(("parallel",)),
        name=name,
    )(x, _warr(w), res, *[g.astype(F32).reshape(1, D) for g in gains])
    return outs


def _bucket_table(num_buckets):
    half = num_buckets // 2
    exact = half // 2
    n = np.arange(MAX_DISTANCE + 1)
    large = exact + (np.log(np.maximum(n, 1).astype(np.float32) / np.float32(exact))
                     / np.float32(math.log(MAX_DISTANCE / exact)) * np.float32(half - exact)).astype(np.int32)
    large = np.minimum(large, half - 1)
    return np.where(n < exact, n, large), half


def _bias_body(rb_ref, out_ref, *, tiles, steps, half, far_bucket):
    h = pl.program_id(0)
    for t, (q0, k0) in enumerate(tiles):
        nq, nk = out_ref.shape[2], out_ref.shape[3]
        qpos = q0 + lax.broadcasted_iota(jnp.int32, (nq, nk), 0)
        kpos = k0 + lax.broadcasted_iota(jnp.int32, (nq, nk), 1)
        rel = kpos - qpos
        n = jnp.abs(rel)
        neg = jnp.full((nq, nk), rb_ref[0, h], F32)
        pos = jnp.full((nq, nk), rb_ref[half, h], F32)
        for thr, b in steps:
            ge = n >= thr
            neg = jnp.where(ge, rb_ref[b, h], neg)
            pos = jnp.where(ge, rb_ref[half + b, h], pos)
        bias = jnp.where(rel > 0, pos, neg) - rb_ref[far_bucket, h]
        visible = kpos // CHUNK <= qpos // CHUNK
        out_ref[t, 0] = jnp.where(visible, bias * LOG2_E, NEG_INF)


def _bias_tiles(rel_bias, tiles, nq, nk):
    NB, H = rel_bias.shape
    table, half = _bucket_table(NB)
    steps = [(int(i), int(table[i])) for i in range(1, len(table)) if table[i] != table[i - 1]]
    return pl.pallas_call(
        functools.partial(_bias_body, tiles=tuple(tiles), steps=tuple(steps), half=half,
                          far_bucket=int(table[-1])),
        grid=(H,),
        in_specs=[pl.BlockSpec(memory_space=pltpu.SMEM)],
        out_specs=pl.BlockSpec((len(tiles), 1, nq, nk), lambda h: (0, h, 0, 0)),
        out_shape=jax.ShapeDtypeStruct((len(tiles), H, nq, nk), F32),
        compiler_params=_cparams(("parallel",)),
        name="t5_bias_tiles",
    )(rel_bias.astype(F32))


def _lane_tile(x, n):
    if n <= LANES:
        return x[:, :n]
    return jnp.concatenate([x] * (n // LANES), axis=1)


def _softmax_update(s, vb, m_ref, l_ref, acc_ref, idx):
    m_prev = m_ref[idx]
    l_prev = l_ref[idx]
    m_new = jnp.maximum(m_prev, jnp.max(s, axis=1, keepdims=True))
    alpha = jnp.exp2(m_prev - m_new)
    p = jnp.exp2(s - _lane_tile(m_new, s.shape[1]))
    l_ref[idx] = alpha * l_prev + jnp.sum(p, axis=1, keepdims=True)
    acc_ref[idx] = acc_ref[idx] * _lane_tile(alpha, vb.shape[1]) + _dot(p.astype(BF16), vb)
    m_ref[idx] = m_new


def _diff_finalize(lam_ref, sn_ref, o_ref, l_ref, acc_ref, *, H, d, lam_init):
    lv = lam_ref[...]
    lam = (jnp.exp(jnp.sum(lv[0:1] * lv[1:2], axis=1, keepdims=True))
           - jnp.exp(jnp.sum(lv[2:3] * lv[3:4], axis=1, keepdims=True)) + lam_init)
    for h in range(H):
        o = (acc_ref[2 * h] / _lane_tile(l_ref[2 * h], 2 * d)
             - lam * (acc_ref[2 * h + 1] / _lane_tile(l_ref[2 * h + 1], 2 * d)))
        on = o * lax.rsqrt(jnp.mean(o * o, axis=-1, keepdims=True) + DIFF_SUBLN_EPS) * sn_ref[...]
        o_ref[:, h * 2 * d:(h + 1) * 2 * d] = (on * (1.0 - lam_init)).astype(o_ref.dtype)


def _attn_body(qi_ref, kj_ref, q_ref, k_ref, v_ref, bias_ref, lam_ref, sn_ref, o_ref, m_ref, l_ref, acc_ref,
               *, H, d, lam_init, kv_mult, n_tiles):
    p = pl.program_id(1)
    qi = qi_ref[p]
    kj = kj_ref[p]

    @pl.when(kj == 0)
    def _():
        m_ref[...] = jnp.full(m_ref.shape, -jnp.inf, F32)
        l_ref[...] = jnp.zeros(l_ref.shape, F32)
        acc_ref[...] = jnp.zeros(acc_ref.shape, F32)

    bidx = qi - kv_mult * kj

    def sweep(near):
        for h in range(H):
            vb = v_ref[:, h * 2 * d:(h + 1) * 2 * d]
            for mp in range(2):
                c0 = (2 * h + mp) * d
                s = _dot_nt(q_ref[:, c0:c0 + d], k_ref[:, c0:c0 + d])
                if near:
                    s = s + bias_ref[bidx, h]
                _softmax_update(s, vb, m_ref, l_ref, acc_ref, 2 * h + mp)

    @pl.when(bidx < n_tiles)
    def _():
        sweep(True)

    @pl.when(bidx >= n_tiles)
    def _():
        sweep(False)

    @pl.when(kj == qi // kv_mult)
    def _():
        _diff_finalize(lam_ref, sn_ref, o_ref, l_ref, acc_ref, H=H, d=d, lam_init=lam_init)


def _diff_attention_prompt(q, k, v, rel_bias, lam_vecs, sub_norm, lam_init, *, nseq, d):
    T, QW = q.shape
    H = QW // (2 * d)
    L = T // nseq
    tq = _tile(L, 256, LANES)
    kv_mult = 2 if L % (2 * tq) == 0 else 1
    tk = kv_mult * tq
    assert tq % CHUNK == 0 and L % tq == 0
    nq, nk = L // tq, L // tk
    pairs = [(i, j) for i in range(nq) for j in range(i // kv_mult + 1)]
    qi = jnp.asarray([pq for pq, _ in pairs], jnp.int32)
    kj = jnp.asarray([pk for _, pk in pairs], jnp.int32)
    n_tiles = -(-(tk + MAX_DISTANCE - 1) // tq)
    bias = _bias_tiles(rel_bias, [(t * tq, 0) for t in range(n_tiles)], tq, tk)
    qrow = lambda b, p, qi_r, kj_r: (b * nq + qi_r[p], 0)
    krow = lambda b, p, qi_r, kj_r: (b * nk + kj_r[p], 0)
    fixed2 = lambda b, p, qi_r, kj_r: (0, 0)
    grid_spec = pltpu.PrefetchScalarGridSpec(
        num_scalar_prefetch=2,
        grid=(nseq, len(pairs)),
        in_specs=[
            pl.BlockSpec((tq, QW), qrow),
            pl.BlockSpec((tk, QW), krow),
            pl.BlockSpec((tk, QW), krow),
            pl.BlockSpec((n_tiles, H, tq, tk), lambda b, p, qi_r, kj_r: (0, 0, 0, 0),
                         pipeline_mode=pl.Buffered(1)),
            pl.BlockSpec((4, d), fixed2),
            pl.BlockSpec((1, 2 * d), fixed2),
        ],
        out_specs=pl.BlockSpec((tq, QW), qrow),
        scratch_shapes=[pltpu.VMEM((2 * H, tq, LANES), F32), pltpu.VMEM((2 * H, tq, LANES), F32),
                        pltpu.VMEM((2 * H, tq, 2 * d), F32)],
    )
    return pl.pallas_call(
        functools.partial(_attn_body, H=H, d=d, lam_init=lam_init, kv_mult=kv_mult, n_tiles=n_tiles),
        grid_spec=grid_spec,
        out_shape=jax.ShapeDtypeStruct((T, QW), BF16),
        compiler_params=_cparams(("parallel", "arbitrary")),
        name="diff_attention_prompt",
    )(qi, kj, q, k, v, bias, lam_vecs, sub_norm.astype(F32).reshape(1, 2 * d))


def _attn_dec_body(q_ref, ck_ref, cv_ref, kn_ref, vn_ref, bc_ref, bn_ref, lam_ref, sn_ref, o_ref,
                   m_ref, l_ref, acc_ref, *, H, d, lam_init, n_blocks):
    kb_i = pl.program_id(1)

    @pl.when(kb_i == 0)
    def _():
        m_ref[...] = jnp.full(m_ref.shape, -jnp.inf, F32)
        l_ref[...] = jnp.zeros(l_ref.shape, F32)
        acc_ref[...] = jnp.zeros(acc_ref.shape, F32)

    def sweep(k_of, v_of, b_ref):
        for h in range(H):
            vb = v_of(h)
            for mp in range(2):
                c = 2 * h + mp
                s = _dot_nt(q_ref[:, c * d:(c + 1) * d], k_of(c)) + b_ref[h]
                _softmax_update(s, vb, m_ref, l_ref, acc_ref, c)

    tk = ck_ref.shape[1] // (2 * H)
    sweep(lambda c: ck_ref[0, pl.ds(c, tk, stride=2 * H), :].astype(BF16),
          lambda h: jnp.concatenate([cv_ref[0, pl.ds(half * H + h, tk, stride=2 * H), :] for half in range(2)],
                                    axis=1).astype(BF16), bc_ref)

    @pl.when(kb_i == n_blocks - 1)
    def _():
        sweep(lambda c: kn_ref[:, c * d:(c + 1) * d], lambda h: vn_ref[:, h * 2 * d:(h + 1) * 2 * d], bn_ref)
        _diff_finalize(lam_ref, sn_ref, o_ref, l_ref, acc_ref, H=H, d=d, lam_init=lam_init)


def _diff_attention_decode(q, k_new, v_new, cache_k, cache_v, rel_bias, lam_vecs, sub_norm, lam_init, *, d):
    B, P, H = cache_v.shape[:3]
    QW = H * 2 * d
    Lq = q.shape[0] // B
    tk = _tile(P, 512, LANES)
    nb = P // tk
    cache_v_rows = cache_v.reshape(B, P, H, 2, d).transpose(0, 1, 3, 2, 4).reshape(B, P * 2 * H, d)
    bias_c = _bias_tiles(rel_bias, [(P, j * tk) for j in range(nb)], Lq, tk)
    bias_n = _bias_tiles(rel_bias, [(P, P)], Lq, Lq)[0]
    seq = lambda b, j: (b, 0)
    fixed2 = lambda b, j: (0, 0)
    return pl.pallas_call(
        functools.partial(_attn_dec_body, H=H, d=d, lam_init=lam_init, n_blocks=nb),
        grid=(B, nb),
        in_specs=[
            pl.BlockSpec((Lq, QW), seq),
            pl.BlockSpec((1, tk * 2 * H, d), lambda b, j: (b, j, 0)),
            pl.BlockSpec((1, tk * 2 * H, d), lambda b, j: (b, j, 0)),
            pl.BlockSpec((Lq, QW), seq),
            pl.BlockSpec((Lq, QW), seq),
            pl.BlockSpec((None, H, Lq, tk), lambda b, j: (j, 0, 0, 0)),
            pl.BlockSpec((H, Lq, Lq), lambda b, j: (0, 0, 0)),
            pl.BlockSpec((4, d), fixed2),
            pl.BlockSpec((1, 2 * d), fixed2),
        ],
        out_specs=pl.BlockSpec((Lq, QW), seq),
        out_shape=jax.ShapeDtypeStruct((B * Lq, QW), BF16),
        scratch_shapes=[pltpu.VMEM((2 * H, Lq, LANES), F32), pltpu.VMEM((2 * H, Lq, LANES), F32),
                        pltpu.VMEM((2 * H, Lq, 2 * d), F32)],
        compiler_params=_cparams(("parallel", "arbitrary")),
        name="diff_attention_decode",
    )(q, cache_k.reshape(B, P * 2 * H, d), cache_v_rows, k_new, v_new, bias_c, bias_n,
      lam_vecs, sub_norm.astype(F32).reshape(1, 2 * d))


def _trunk(x, p, st_gdn, st_gdn_conv, st_ffn_conv, cache_k, cache_v):
    B, L, D = x.shape
    fresh = cache_k is None
    T = B * L
    depth = p['f_norm'].shape[0]
    n_a = p['a_norm'].shape[0]
    HV, dv = p['a_log'].shape[1], p['a_out_norm'].shape[1]
    VW = HV * dv
    QKVW = p['a_w_conv'].shape[2]
    dk = p['gdn_head_k']
    d = p['b_lam_q1'].shape[1]
    DFF = p['f_w_down'].shape[1]
    QW = p['b_w_q'].shape[2]

    time_major = not fresh
    if time_major:
        h = x.transpose(1, 0, 2).reshape(T, D)
        stride, nseq_conv = B, 1
    else:
        h = x.reshape(T, D)
        stride, nseq_conv = 1, B

    def to_seq(t):
        return t.reshape(L, B, -1).transpose(1, 0, 2).reshape(T, -1) if time_major else t

    def to_time(t):
        return t.reshape(B, L, -1).transpose(1, 0, 2).reshape(T, -1) if time_major else t

    def conv_state_in(st, width, chans):
        if st is None:
            return jnp.zeros((nseq_conv, (width - 1) * stride, chans), F32)
        return st.transpose(1, 0, 2).reshape(1, (width - 1) * B, chans)

    def conv_state_out(st, width):
        if time_major:
            return st.reshape(width - 1, B, -1).transpose(1, 0, 2)
        return st

    new_S, new_gconv, new_fconv = [], [], []
    k_f32 = v_f32 = k_bf = v_bf = None
    xn = _rmsnorm_cast(h, p['a_norm'][0]) if n_a > 0 else _rmsnorm_cast(h, p['b_norm'][0])
    xn_kv = None
    y = None
    for layer in range(depth):
        if layer < n_a:
            i = layer
            w_in = (i, p['a_w_in_bf'])
            w_ba = p['a_w_in_bf'][i][:, QKVW + VW:]
            gw = p['a_w_conv'].shape[1]
            cst = conv_state_in(None if fresh else st_gdn_conv[i], gw, QKVW)
            qkv, gconv = _conv_matmul(xn, w_in, p['a_w_conv'][i], None, cst, groups=1, width=QKVW,
                                      stride=stride, nseq=nseq_conv, glu=False, name="gdn_qkv_conv",
                                      tm_pref=512, tn_pref=2048)
            (sz,) = _matmul(xn, w_in, lambda acc: (_silu(acc),), [BF16], "gdn_z_proj", col0=QKVW, width=VW)
            beta, g = _gate_proj(xn, w_ba[:, :HV], w_ba[:, HV:], p['a_log'][i], p['a_dt_bias'][i],
                                 stride=stride, chunk=min(CHUNK, L))
            S0 = jnp.zeros((B, HV, dk, dv), F32) if fresh else st_gdn[i]
            og, S = _gdn_core(to_seq(qkv), to_seq(sz), to_seq(beta), to_seq(g), S0, p['a_out_norm'][i],
                              nseq=B, chunk=min(CHUNK, L), hb=min(GDN_HEADS_PER_STEP, HV))
            new_S.append(S)
            new_gconv.append(conv_state_out(gconv, gw))
            h, xn = _matmul_residual(to_time(og), (i, p['a_w_out_bf']), h, [p['f_norm'][layer]],
                                     BF16, True, "gdn_out_proj")
        else:
            j = layer - n_a
            lam_init = 0.8 - 0.6 * math.exp(-0.3 * layer)
            scale = d ** -0.5 * LOG2_E
            (q,) = _matmul(xn, (j, p['b_w_q_bf']), lambda acc: (acc * scale,), [BF16], "diff_q_proj")
            lam_vecs = jnp.stack([p['b_lam_q1'][j], p['b_lam_k1'][j], p['b_lam_q2'][j],
                                  p['b_lam_k2'][j]]).astype(F32)
            if fresh:
                ao = _diff_attention_prompt(q, k_bf, v_bf, p['rel_bias'], lam_vecs, p['b_sub_norm'][j],
                                            lam_init, nseq=B, d=d)
            else:
                ao = _diff_attention_decode(to_seq(q), to_seq(k_bf), to_seq(v_bf),
                                            cache_k, cache_v, p['rel_bias'], lam_vecs, p['b_sub_norm'][j], lam_init, d=d)
                ao = to_time(ao)
            h, xn = _matmul_residual(ao, (j, p['b_w_o_bf']), h, [p['f_norm'][layer]],
                                     BF16, True, "diff_out_proj")
        fw = p['f_w_conv'].shape[1]
        fst = conv_state_in(None if fresh else st_ffn_conv[layer], fw, 2 * DFF)
        act, fconv = _conv_matmul(xn, (layer, p['f_w_up_bf']), p['f_w_conv'][layer],
                                  p['f_b_conv'][layer], fst, groups=2, width=2 * DFF, stride=stride,
                                  nseq=nseq_conv, glu=True, name="ffn_up_conv", tm_pref=512, tn_pref=2816)
        new_fconv.append(conv_state_out(fconv, fw))
        last = layer == depth - 1
        gains = []
        if layer == n_a - 1:
            gains.append(p['kv_norm'])
        if last:
            gains.append(p['final_norm'])
        elif layer + 1 < n_a:
            gains.append(p['a_norm'][layer + 1])
        else:
            gains.append(p['b_norm'][layer + 1 - n_a])
        outs = _matmul_residual(act, (layer, p['f_w_down_bf']), h, gains,
                                F32 if last else BF16, not last, "ffn_down_proj")
        if last:
            if layer == n_a - 1:
                xn_kv = outs[0].astype(BF16)
            y = outs[-1]
        else:
            h = outs[0]
            if layer == n_a - 1:
                xn_kv = outs[1]
            xn = outs[-1]
        if layer == n_a - 1:
            k_f32, k_bf = _head_projection(xn_kv, p['w_kv_bf'], 0, QW, d, "k_proj")
            v_f32, v_bf = _head_projection(xn_kv, p['w_kv_bf'], QW, QW, 2 * d, "v_proj")

    n_kh = 2 * (QW // (2 * d))
    y = to_seq(y).reshape(B, L, D)
    k_sh = to_seq(k_f32).reshape(B, L, n_kh, d)
    v_sh = to_seq(v_f32).reshape(B, L, n_kh // 2, 2 * d)
    return y, jnp.stack(new_S), jnp.stack(new_gconv), jnp.stack(new_fconv), k_sh, v_sh


def kernel(x_prompt, x_sample, state_gdn, state_gdn_conv, state_ffn_conv, cache_k, cache_v, a_norm, a_w_in, a_w_conv, a_log, a_dt_bias, a_out_norm, a_w_out, kv_norm, w_kv, b_norm, b_w_q, b_lam_q1, b_lam_k1, b_lam_q2, b_lam_k2, b_sub_norm, b_w_o, rel_bias, f_norm, f_w_up, f_w_conv, f_b_conv, f_w_down, final_norm):
    p = {
        'a_norm': a_norm, 'a_w_in': a_w_in, 'a_w_conv': a_w_conv, 'a_log': a_log,
        'a_dt_bias': a_dt_bias, 'a_out_norm': a_out_norm, 'a_w_out': a_w_out,
        'kv_norm': kv_norm, 'w_kv': w_kv,
        'b_norm': b_norm, 'b_w_q': b_w_q, 'b_lam_q1': b_lam_q1, 'b_lam_k1': b_lam_k1,
        'b_lam_q2': b_lam_q2, 'b_lam_k2': b_lam_k2, 'b_sub_norm': b_sub_norm, 'b_w_o': b_w_o,
        'rel_bias': rel_bias,
        'f_norm': f_norm, 'f_w_up': f_w_up, 'f_w_conv': f_w_conv, 'f_b_conv': f_b_conv,
        'f_w_down': f_w_down, 'final_norm': final_norm,
        'gdn_head_k': state_gdn.shape[3],
    }
    for name in ('a_w_in', 'a_w_out', 'w_kv', 'b_w_q', 'b_w_o', 'f_w_up', 'f_w_down'):
        p[name + '_bf'] = p[name].astype(BF16)
    out_p = _trunk(x_prompt, p, None, None, None, None, None)
    out_s = _trunk(x_sample, p, state_gdn, state_gdn_conv, state_ffn_conv, cache_k, cache_v)
    return (out_p[0], out_s[0]) + tuple(out_p[1:]) + tuple(out_s[1:])
```

```python
import functools
import math

import numpy as np
import jax
import jax.numpy as jnp
from jax import lax
from jax.experimental import pallas as pl
from jax.experimental.pallas import tpu as pltpu

F32 = jnp.float32
BF16 = jnp.bfloat16

CHUNK = 64
NORM_EPS = 1e-6
DIFF_SUBLN_EPS = 1e-5
L2_EPS = 1e-6
MAX_DISTANCE = 128
NEG_INF = -1e30
LOG2_E = math.log2(math.e)
GDN_HEADS_PER_STEP = 32

V7X_VMEM_LIMIT = 56 * 1024 * 1024
RESIDUAL_MATMUL_VMEM_BUDGET = 46 * 1024 * 1024
LANES = 128
V7X_MXU_COLS = 256
F32_SUBLANES = 8
BF16_SUBLANES = 16


def _cparams(sem):
    return pltpu.CompilerParams(dimension_semantics=sem, vmem_limit_bytes=V7X_VMEM_LIMIT)


def _tile(n, pref, mult):
    if n <= pref:
        return n
    t = (pref // mult) * mult
    while t >= mult:
        if n % t == 0:
            return t
        t -= mult
    return n


def _wspec(w, block, index_map, **kw):
    if isinstance(w, tuple):
        layer = w[0]
        return pl.BlockSpec((None,) + block, lambda *a: (layer,) + index_map(*a), **kw)
    return pl.BlockSpec(block, index_map, **kw)


def _warr(w):
    return w[1] if isinstance(w, tuple) else w


def _dot(a, b):
    return jnp.dot(a, b, preferred_element_type=F32)


def _dot_nt(a, b):
    return lax.dot_general(a, b, (((1,), (1,)), ((), ())), preferred_element_type=F32)


def _dot_tn(a, b):
    return lax.dot_general(a, b, (((0,), (0,)), ((), ())), preferred_element_type=F32)


def _dot_hi(a, b):
    return jnp.dot(a, b, preferred_element_type=F32, precision=lax.Precision.HIGHEST)


def _split_bf16(x):
    bits = lax.bitcast_convert_type(x, jnp.uint32) & jnp.uint32(0xFFFF0000)
    hi = lax.bitcast_convert_type(bits, F32)
    return hi, x - hi


def _x4_lhs(x, parts=None):
    xh, xl = parts or _split_bf16(x)
    xx = jnp.concatenate([xh, xl], axis=1).astype(BF16)
    return jnp.concatenate([xx, xx], axis=1)


def _x4_rhs(y, parts=None):
    yh, yl = parts or _split_bf16(y)
    return jnp.concatenate([yh, yh, yl, yl], axis=0).astype(BF16)


def _x4_both(x):
    parts = _split_bf16(x)
    return _x4_lhs(x, parts), _x4_rhs(x, parts)


def _dot_x4(lhs4, rhs4):
    return _dot(lhs4, rhs4)


def _silu(x):
    h = 0.5 * x
    return h + h * jnp.tanh(h)


def _rms_body(x_ref, g_ref, o_ref):
    x = x_ref[...]
    ms = jnp.mean(x * x, axis=-1, keepdims=True)
    o_ref[...] = (x * lax.rsqrt(ms + NORM_EPS) * g_ref[...]).astype(o_ref.dtype)


def _rmsnorm_cast(x, g):
    T, D = x.shape
    tm = _tile(T, 512, BF16_SUBLANES)
    return pl.pallas_call(
        _rms_body,
        grid=(T // tm,),
        in_specs=[pl.BlockSpec((tm, D), lambda i: (i, 0)), pl.BlockSpec((1, D), lambda i: (0, 0))],
        out_specs=pl.BlockSpec((tm, D), lambda i: (i, 0)),
        out_shape=jax.ShapeDtypeStruct((T, D), BF16),
        compiler_params=_cparams(("parallel",)),
        name="rmsnorm_cast",
    )(x, g.reshape(1, D).astype(F32))


def _mm_body(x_ref, w_ref, *o_refs, epilogue):
    acc = _dot(x_ref[...], w_ref[...])
    for o_ref, val in zip(o_refs, epilogue(acc)):
        o_ref[...] = val.astype(o_ref.dtype)


def _matmul(x, w, epilogue, out_dtypes, name, col0=0, width=None, tm_pref=1024, tn_pref=1024):
    T, K = x.shape
    N = _warr(w).shape[-1] if width is None else width
    tm = _tile(T, tm_pref, BF16_SUBLANES)
    tn = _tile(math.gcd(N, col0) if col0 else N, tn_pref, LANES)
    cb = col0 // tn
    outs = pl.pallas_call(
        functools.partial(_mm_body, epilogue=epilogue),
        grid=(N // tn, T // tm),
        in_specs=[pl.BlockSpec((tm, K), lambda n, m: (m, 0)), _wspec(w, (K, tn), lambda n, m: (0, n + cb))],
        out_specs=[pl.BlockSpec((tm, tn), lambda n, m: (m, n)) for _ in out_dtypes],
        out_shape=[jax.ShapeDtypeStruct((T, N), dt) for dt in out_dtypes],
        compiler_params=_cparams(("parallel", "parallel")),
        name=name,
    )(x, _warr(w))
    return outs


def _headproj_body(x_ref, w_ref, o3_ref, obf_ref, *, hd, features_first):
    acc = _dot(x_ref[...], w_ref[...])
    obf_ref[...] = (acc.T if features_first else acc).astype(obf_ref.dtype)
    for hh in range(o3_ref.shape[1]):
        o3_ref[:, hh, :] = acc[:, hh * hd:(hh + 1) * hd]


def _head_projection(x, w, col0, width, hd, name, features_first=False):
    T, K = x.shape
    nh = width // hd
    hb = nh if nh <= F32_SUBLANES else F32_SUBLANES
    tn = hb * hd
    tm = _tile(T, 512, LANES)
    cb = col0 // tn
    if features_first:
        bf_spec = pl.BlockSpec((tn, tm), lambda n, m: (n, m))
        bf_shape = jax.ShapeDtypeStruct((width, T), BF16)
    else:
        bf_spec = pl.BlockSpec((tm, tn), lambda n, m: (m, n))
        bf_shape = jax.ShapeDtypeStruct((T, width), BF16)
    return pl.pallas_call(
        functools.partial(_headproj_body, hd=hd, features_first=features_first),
        grid=(width // tn, T // tm),
        in_specs=[pl.BlockSpec((tm, K), lambda n, m: (m, 0)), pl.BlockSpec((K, tn), lambda n, m: (0, n + cb))],
        out_specs=[pl.BlockSpec((tm, hb, hd), lambda n, m: (m, n, 0)), bf_spec],
        out_shape=[jax.ShapeDtypeStruct((T, nh, hd), F32), bf_shape],
        compiler_params=_cparams(("parallel", "parallel")),
        name=name,
    )(x, w)


def _gate_body(x_ref, wb_ref, wa_ref, alog_ref, dtb_ref, beta_ref, gcum_ref, *, stride, chunk):
    x = x_ref[...]
    b = _dot(x, wb_ref[...])
    a = _dot(x, wa_ref[...]) + dtb_ref[...]
    beta_ref[...] = 1.0 / (1.0 + jnp.exp(-b))
    softplus = jnp.maximum(a, 0.0) + jnp.log(1.0 + jnp.exp(-jnp.abs(a)))
    g = -jnp.exp(alog_ref[...]) * softplus
    bs = stride * chunk
    ri = lax.broadcasted_iota(jnp.int32, (bs, bs), 0)
    ci = lax.broadcasted_iota(jnp.int32, (bs, bs), 1)
    earlier = jnp.logical_and(ci <= ri, (ri - ci) % stride == 0).astype(F32)
    for r0 in range(0, x.shape[0], bs):
        gcum_ref[r0:r0 + bs, :] = _dot_hi(earlier, g[r0:r0 + bs])


def _gate_proj(xn, w_b, w_a, a_log, dt_bias, *, stride, chunk):
    T, K = xn.shape
    H = w_b.shape[1]
    pad = LANES - H
    wb = jnp.pad(w_b, ((0, 0), (0, pad))).astype(BF16)
    wa = jnp.pad(w_a, ((0, 0), (0, pad))).astype(BF16)
    al = jnp.pad(a_log.astype(F32), (0, pad)).reshape(1, LANES)
    db = jnp.pad(dt_bias.astype(F32), (0, pad)).reshape(1, LANES)
    tm = _tile(T, 1024, stride * chunk)
    row = pl.BlockSpec((tm, K), lambda i: (i, 0))
    wsp = pl.BlockSpec((K, LANES), lambda i: (0, 0))
    vsp = pl.BlockSpec((1, LANES), lambda i: (0, 0))
    osp = pl.BlockSpec((tm, LANES), lambda i: (i, 0))
    beta, gcum = pl.pallas_call(
        functools.partial(_gate_body, stride=stride, chunk=chunk),
        grid=(T // tm,),
        in_specs=[row, wsp, wsp, vsp, vsp],
        out_specs=[osp, osp],
        out_shape=[jax.ShapeDtypeStruct((T, LANES), F32)] * 2,
        compiler_params=_cparams(("parallel",)),
        name="gdn_gate_proj",
    )(xn, wb, wa, al, db)
    return beta[:, :H], gcum[:, :H]


def _convmm_body(*refs, G, W, stride, tm, pad, tps, glu, n_sub):
    hist = (W - 1) * stride
    it = iter(refs)
    x_ref = next(it)
    w_refs = [next(it) for _ in range(G)]
    wc_refs = [next(it) for _ in range(G)]
    b_refs = [next(it) for _ in range(G)] if glu else None
    st_refs = [next(it) for _ in range(G)]
    out_ref = next(it)
    nst_refs = [next(it) for _ in range(G)]
    yscs = [[next(it) for _ in range(n_sub)] for _ in range(G)]

    m = pl.program_id(1)
    first = (m % tps) == 0
    last = (m % tps) == tps - 1
    sub = out_ref.shape[1] // n_sub

    @pl.when(first)
    def _():
        for g in range(G):
            for c in range(n_sub):
                yscs[g][c][pad - hist:pad, :] = st_refs[g][:, c * sub:(c + 1) * sub]

    @pl.when(jnp.logical_not(first))
    def _():
        for g in range(G):
            for c in range(n_sub):
                yscs[g][c][0:pad, :] = yscs[g][c][tm:tm + pad, :]

    def project(c):
        cols = slice(c * sub, (c + 1) * sub)
        for g in range(G):
            yscs[g][c][pad:pad + tm, :] = _dot(x_ref[...], w_refs[g][:, cols])

    def conv_act(c):
        cols = slice(c * sub, (c + 1) * sub)
        convs = []
        for g in range(G):
            acc = None
            for i in range(W):
                off = pad - (W - 1 - i) * stride
                term = yscs[g][c][off:off + tm, :] * wc_refs[g][i:i + 1, cols]
                acc = term if acc is None else acc + term
            if glu:
                acc = acc + b_refs[g][:, cols]
            convs.append(acc)
        res = _silu(convs[0]) * convs[1] if glu else _silu(convs[0])
        out_ref[:, cols] = res.astype(out_ref.dtype)

    project(0)
    for c in range(1, n_sub):
        project(c)
        conv_act(c - 1)
    conv_act(n_sub - 1)

    @pl.when(last)
    def _():
        for g in range(G):
            for c in range(n_sub):
                nst_refs[g][:, c * sub:(c + 1) * sub] = yscs[g][c][pad + tm - hist:pad + tm, :]


def _conv_matmul(x, w, wc, bias, state, *, groups, width, stride, nseq, glu, name, tm_pref, tn_pref):
    T, K = x.shape
    W = wc.shape[0]
    Ng = width // groups
    hist = (W - 1) * stride
    rows = T // nseq
    tm = _tile(rows, tm_pref, BF16_SUBLANES)
    tn = _tile(Ng, tn_pref, V7X_MXU_COLS)
    n_sub = max(tn // V7X_MXU_COLS, 1)
    tps = rows // tm
    pad = -(-hist // F32_SUBLANES) * F32_SUBLANES
    assert tm >= pad and state.shape == (nseq, hist, groups * Ng)
    nb = Ng // tn

    def col(g):
        return lambda n, m: (0, n + g * nb)

    def stcol(g):
        return lambda n, m: (m // tps, 0, n + g * nb)

    in_specs = [pl.BlockSpec((tm, K), lambda n, m: (m, 0))]
    args = [x]
    in_specs += [_wspec(w, (K, tn), col(g), pipeline_mode=pl.Buffered(1)) for g in range(groups)]
    args += [_warr(w)] * groups
    in_specs += [pl.BlockSpec((W, tn), col(g)) for g in range(groups)]
    args += [wc.astype(F32)] * groups
    if glu:
        in_specs += [pl.BlockSpec((1, tn), col(g)) for g in range(groups)]
        args += [bias.astype(F32).reshape(1, -1)] * groups
    in_specs += [pl.BlockSpec((None, hist, tn), stcol(g)) for g in range(groups)]
    args += [state.astype(F32)] * groups
    out_specs = [pl.BlockSpec((tm, tn), lambda n, m: (m, n))]
    out_specs += [pl.BlockSpec((None, hist, tn), stcol(0)) for g in range(groups)]
    out_shape = [jax.ShapeDtypeStruct((T, Ng), BF16)]
    out_shape += [jax.ShapeDtypeStruct((nseq, hist, Ng), F32) for g in range(groups)]
    outs = pl.pallas_call(
        functools.partial(_convmm_body, G=groups, W=W, stride=stride, tm=tm, pad=pad, tps=tps, glu=glu,
                          n_sub=n_sub),
        grid=(nb, T // tm),
        in_specs=in_specs,
        out_specs=out_specs,
        out_shape=out_shape,
        scratch_shapes=[pltpu.VMEM((pad + tm, tn // n_sub), F32) for _ in range(groups * n_sub)],
        compiler_params=_cparams(("parallel", "arbitrary")),
        name=name,
    )(*args)
    new_state = outs[1] if groups == 1 else jnp.concatenate(outs[1:], axis=-1)
    return outs[0], new_state


def _gdn_body(q_ref, k_ref, v_ref, z_ref, beta_ref, g_ref, s0_ref, gn_ref, o_ref, sout_ref, S,
              *, hb, rep, dk, dv, n_chunks):
    n = pl.program_id(2)

    @pl.when(n == 0)
    def _():
        S[...] = s0_ref[0]

    C = q_ref.shape[0]
    ri = lax.broadcasted_iota(jnp.int32, (C, C), 0)
    ci = lax.broadcasted_iota(jnp.int32, (C, C), 1)
    incl = ri >= ci
    strict = ri > ci
    eye = (ri == ci).astype(F32)
    Gc = g_ref[...]
    beta = beta_ref[...]
    gain = gn_ref[...]

    n_sq = int(round(math.log2(C))) - 1
    heads = range(hb)
    kn, qn, kk, qk = [], [], [], []
    for jk in range(hb // rep):
        kf = k_ref[:, jk * dk:(jk + 1) * dk].astype(F32)
        qf = q_ref[:, jk * dk:(jk + 1) * dk].astype(F32)
        kn.append(kf * lax.rsqrt(jnp.sum(kf * kf, axis=-1, keepdims=True) + L2_EPS))
        qn.append(qf * lax.rsqrt(jnp.sum(qf * qf, axis=-1, keepdims=True) + L2_EPS) * (dk ** -0.5))
        kb = kn[jk].astype(BF16)
        kq = _dot_nt(jnp.concatenate([kb, qn[jk].astype(BF16)], axis=0), kb)
        kk.append(kq[:C])
        qk.append(kq[C:])
    Gcol = [Gc[:, j:j + 1] for j in heads]
    bcol = [beta[:, j:j + 1] for j in heads]
    dec_incl, Nm = [], []
    for j in heads:
        Grow = jnp.sum(eye * Gcol[j], axis=0, keepdims=True)
        diff = Gcol[j] - Grow
        dec = jnp.where(incl, jnp.exp(jnp.where(incl, diff, 0.0)), 0.0)
        dec_incl.append(dec)
        Nm.append(-(bcol[j] * kk[j // rep] * jnp.where(strict, dec, 0.0)))
    Tm = [eye + Nm[j] for j in heads]
    Np = [_dot_x4(*_x4_both(Nm[j])) for j in heads]
    for lvl in range(n_sq):
        nparts = [_split_bf16(Np[j]) for j in heads]
        rhs4 = [_x4_rhs(None, nparts[j]) for j in heads]
        tparts = [_split_bf16(Tm[j]) for j in heads]
        if lvl + 1 < n_sq:
            stacked = [tuple(jnp.concatenate([tp, npart], axis=0) for tp, npart in zip(tparts[j], nparts[j]))
                       for j in heads]
            res = [_dot_x4(_x4_lhs(None, stacked[j]), rhs4[j]) for j in heads]
            Tm = [Tm[j] + res[j][:C] for j in heads]
            Np = [res[j][C:] for j in heads]
        else:
            Tm = [Tm[j] + _dot_x4(_x4_lhs(None, tparts[j]), rhs4[j]) for j in heads]
    eG = [jnp.exp(Gcol[j]) for j in heads]
    sol = []
    for j in heads:
        vf = v_ref[:, j * dv:(j + 1) * dv].astype(F32)
        rhs = jnp.concatenate([bcol[j] * vf, (bcol[j] * eG[j]) * kn[j // rep]], axis=1)
        sol.append(_dot_x4(_x4_lhs(Tm[j]), _x4_rhs(rhs)))
    Glast = [Gc[C - 1:C, j:j + 1] for j in heads]
    Sold = [S[j] for j in heads]
    Sb = [Sold[j].astype(BF16) for j in heads]
    wq, pk = [], []
    for j in heads:
        q_g = qn[j // rep] * eG[j]
        wq.append(jnp.concatenate([sol[j][:, dv:].astype(BF16), q_g.astype(BF16)], axis=0))
        P = qk[j // rep] * dec_incl[j]
        k_d = kn[j // rep] * jnp.exp(Glast[j] - Gcol[j])
        pk.append(jnp.concatenate([P.astype(BF16), k_d.T.astype(BF16)], axis=0))
    wqS = [_dot(wq[j], Sb[j]) for j in heads]
    ub = [(sol[j][:, :dv] - wqS[j][:C]).astype(BF16) for j in heads]
    pku = [_dot(pk[j], ub[j]) for j in heads]
    for j in heads:
        S[j] = jnp.exp(Glast[j]) * Sold[j] + pku[j][C:]
    for j in heads:
        o = wqS[j][C:] + pku[j][:C]
        on = o * lax.rsqrt(jnp.mean(o * o, axis=-1, keepdims=True) + NORM_EPS) * gain
        o_ref[:, j * dv:(j + 1) * dv] = (on * z_ref[:, j * dv:(j + 1) * dv].astype(F32)).astype(o_ref.dtype)

    @pl.when(n == n_chunks - 1)
    def _():
        sout_ref[0] = S[...]


def _gdn_core(qkv, sz, beta, g, S0, out_norm, *, nseq, chunk, hb):
    T = qkv.shape[0]
    _, HV, dk, dv = S0.shape
    VW = HV * dv
    QK = (qkv.shape[1] - VW) // 2
    HK = QK // dk
    rep = HV // HK
    HG = HV // hb
    kb = hb // rep
    L = T // nseq
    n_chunks = L // chunk

    def group_major(t):
        t = t.reshape(T, HG, hb).transpose(1, 0, 2)
        return jnp.pad(t, ((0, 0), (0, 0), (0, LANES - hb)))

    rowblk = lambda b, h, n: b * n_chunks + n
    in_specs = [
        pl.BlockSpec((chunk, kb * dk), lambda b, h, n: (rowblk(b, h, n), h)),
        pl.BlockSpec((chunk, kb * dk), lambda b, h, n: (rowblk(b, h, n), HG + h)),
        pl.BlockSpec((chunk, hb * dv), lambda b, h, n: (rowblk(b, h, n), 2 * QK // (hb * dv) + h)),
        pl.BlockSpec((chunk, hb * dv), lambda b, h, n: (rowblk(b, h, n), h)),
        pl.BlockSpec((None, chunk, LANES), lambda b, h, n: (h, rowblk(b, h, n), 0)),
        pl.BlockSpec((None, chunk, LANES), lambda b, h, n: (h, rowblk(b, h, n), 0)),
        pl.BlockSpec((1, hb, dk, dv), lambda b, h, n: (b, h, 0, 0)),
        pl.BlockSpec((1, dv), lambda b, h, n: (0, 0)),
    ]
    out_specs = [
        pl.BlockSpec((chunk, hb * dv), lambda b, h, n: (rowblk(b, h, n), h)),
        pl.BlockSpec((1, hb, dk, dv), lambda b, h, n: (b, h, 0, 0)),
    ]
    o, S = pl.pallas_call(
        functools.partial(_gdn_body, hb=hb, rep=rep, dk=dk, dv=dv, n_chunks=n_chunks),
        grid=(nseq, HG, n_chunks),
        in_specs=in_specs,
        out_specs=out_specs,
        out_shape=[jax.ShapeDtypeStruct((T, VW), BF16), jax.ShapeDtypeStruct(S0.shape, F32)],
        scratch_shapes=[pltpu.VMEM((hb, dk, dv), F32)],
        compiler_params=_cparams(("parallel", "parallel", "arbitrary")),
        name="gdn_core",
    )(qkv, qkv, qkv, sz, group_major(beta), group_major(g), S0.astype(F32),
      out_norm.astype(F32).reshape(1, dv))
    return o, S


def _mmres_body(x_ref, w_ref, res_ref, *refs, n_norm, with_h):
    gains = refs[:n_norm]
    outs = refs[n_norm:]
    h = res_ref[...] + _dot(x_ref[...], w_ref[...])
    k = 0
    if with_h:
        outs[0][...] = h
        k = 1
    if n_norm:
        inv = lax.rsqrt(jnp.mean(h * h, axis=-1, keepdims=True) + NORM_EPS)
        y = h * inv
        for i in range(n_norm):
            outs[k + i][...] = (y * gains[i][...]).astype(outs[k + i].dtype)


def _matmul_residual(x, w, res, gains, norm_dtype, with_h, name):
    T, K = x.shape
    D = _warr(w).shape[-1]
    n_f32 = (1 if with_h else 0) + (len(gains) if norm_dtype == F32 else 0)
    n_bf16 = len(gains) if norm_dtype == BF16 else 0

    def vmem_bytes(rows):
        return K * D * 2 + 2 * rows * (K * 2 + D * 4 + n_f32 * D * 4 + n_bf16 * D * 2)

    tm = next(t for t in (1024, 512, 256, 128, BF16_SUBLANES)
              if t == BF16_SUBLANES or vmem_bytes(t) <= RESIDUAL_MATMUL_VMEM_BUDGET)
    tm = _tile(T, tm, BF16_SUBLANES)
    row = lambda i: (i, 0)
    fixed = lambda i: (0, 0)
    n_norm = len(gains)
    in_specs = [pl.BlockSpec((tm, K), row),
                _wspec(w, (K, D), fixed, pipeline_mode=pl.Buffered(1)),
                pl.BlockSpec((tm, D), row)]
    in_specs += [pl.BlockSpec((1, D), fixed) for _ in gains]
    out_shape = ([jax.ShapeDtypeStruct((T, D), F32)] if with_h else []) \
        + [jax.ShapeDtypeStruct((T, D), norm_dtype) for _ in gains]
    outs = pl.pallas_call(
        functools.partial(_mmres_body, n_norm=n_norm, with_h=with_h),
        grid=(T // tm,),
        in_specs=in_specs,
        out_specs=[pl.BlockSpec((tm, D), row) for _ in out_shape],
        out_shape=out_shape,
        compiler_params=_cparams(("parallel",)),
        name=name,
    )(x, _warr(w), res, *[g.astype(F32).reshape(1, D) for g in gains])
    return outs


def _bucket_table(num_buckets):
    half = num_buckets // 2
    exact = half // 2
    n = np.arange(MAX_DISTANCE + 1)
    large = exact + (np.log(np.maximum(n, 1).astype(np.float32) / np.float32(exact))
                     / np.float32(math.log(MAX_DISTANCE / exact)) * np.float32(half - exact)).astype(np.int32)
    large = np.minimum(large, half - 1)
    return np.where(n < exact, n, large), half


def _bias_body(rb_ref, out_ref, *, tiles, steps, half, far_bucket, keys_first):
    h = pl.program_id(0)
    shape = out_ref.shape[2:]
    q_axis, k_axis = (1, 0) if keys_first else (0, 1)
    for t, (q0, k0) in enumerate(tiles):
        qpos = q0 + lax.broadcasted_iota(jnp.int32, shape, q_axis)
        kpos = k0 + lax.broadcasted_iota(jnp.int32, shape, k_axis)
        rel = kpos - qpos
        n = jnp.abs(rel)
        neg = jnp.full(shape, rb_ref[0, h], F32)
        pos = jnp.full(shape, rb_ref[half, h], F32)
        for thr, b in steps:
            ge = n >= thr
            neg = jnp.where(ge, rb_ref[b, h], neg)
            pos = jnp.where(ge, rb_ref[half + b, h], pos)
        bias = jnp.where(rel > 0, pos, neg) - rb_ref[far_bucket, h]
        visible = kpos // CHUNK <= qpos // CHUNK
        out_ref[t, 0] = jnp.where(visible, bias * LOG2_E, NEG_INF)


def _bias_tiles(rel_bias, tiles, nq, nk, keys_first=False):
    NB, H = rel_bias.shape
    table, half = _bucket_table(NB)
    steps = [(int(i), int(table[i])) for i in range(1, len(table)) if table[i] != table[i - 1]]
    shape = (nk, nq) if keys_first else (nq, nk)
    return pl.pallas_call(
        functools.partial(_bias_body, tiles=tuple(tiles), steps=tuple(steps), half=half,
                          far_bucket=int(table[-1]), keys_first=keys_first),
        grid=(H,),
        in_specs=[pl.BlockSpec(memory_space=pltpu.SMEM)],
        out_specs=pl.BlockSpec((len(tiles), 1) + shape, lambda h: (0, h, 0, 0)),
        out_shape=jax.ShapeDtypeStruct((len(tiles), H) + shape, F32),
        compiler_params=_cparams(("parallel",)),
        name="t5_bias_tiles",
    )(rel_bias.astype(F32))


def _lane_tile(x, n):
    if n <= LANES:
        return x[:, :n]
    return jnp.concatenate([x] * (n // LANES), axis=1)


def _softmax_update(s, vb, m_ref, l_ref, acc_ref, idx):
    m_prev = m_ref[idx]
    l_prev = l_ref[idx]
    m_new = jnp.maximum(m_prev, jnp.max(s, axis=1, keepdims=True))
    alpha = jnp.exp2(m_prev - m_new)
    p = jnp.exp2(s - _lane_tile(m_new, s.shape[1]))
    l_ref[idx] = alpha * l_prev + jnp.sum(p, axis=1, keepdims=True)
    acc_ref[idx] = acc_ref[idx] * _lane_tile(alpha, vb.shape[1]) + _dot(p.astype(BF16), vb)
    m_ref[idx] = m_new


def _diff_lambda(lam_ref, lam_init):
    lv = lam_ref[...]
    return (jnp.exp(jnp.sum(lv[0:1] * lv[1:2], axis=1, keepdims=True))
            - jnp.exp(jnp.sum(lv[2:3] * lv[3:4], axis=1, keepdims=True)) + lam_init)


def _subnorm_store(o, h, sn_ref, o_ref, lam_init):
    w = o.shape[1]
    on = o * lax.rsqrt(jnp.mean(o * o, axis=-1, keepdims=True) + DIFF_SUBLN_EPS) * sn_ref[...]
    o_ref[:, h * w:(h + 1) * w] = (on * (1.0 - lam_init)).astype(o_ref.dtype)


def _diff_finalize(lam_ref, sn_ref, o_ref, l_ref, acc_ref, *, H, d, lam_init):
    lam = _diff_lambda(lam_ref, lam_init)
    for h in range(H):
        o = (acc_ref[2 * h] / _lane_tile(l_ref[2 * h], 2 * d)
             - lam * (acc_ref[2 * h + 1] / _lane_tile(l_ref[2 * h + 1], 2 * d)))
        _subnorm_store(o, h, sn_ref, o_ref, lam_init)


def _attn_body(qi_ref, kj_ref, q_ref, k_ref, vt_ref, bias_ref, lam_ref, sn_ref, o_ref, m_ref, l_ref, acc_ref,
               *, H, d, lam_init, kv_mult, n_tiles):
    p = pl.program_id(1)
    qi = qi_ref[p]
    kj = kj_ref[p]

    @pl.when(kj == 0)
    def _():
        m_ref[...] = jnp.full(m_ref.shape, -jnp.inf, F32)
        l_ref[...] = jnp.zeros(l_ref.shape, F32)
        acc_ref[...] = jnp.zeros(acc_ref.shape, F32)

    bidx = qi - kv_mult * kj

    def logits(c):
        return _dot_nt(k_ref[:, c * d:(c + 1) * d], q_ref[:, c * d:(c + 1) * d])

    def sweep(near):
        s_next = logits(0)
        for c in range(2 * H):
            h = c // 2
            s = s_next
            if c + 1 < 2 * H:
                s_next = logits(c + 1)
            if near:
                s = s + bias_ref[bidx, h]
            m_prev = m_ref[c]
            m_new = jnp.maximum(m_prev, jnp.max(s, axis=0, keepdims=True))
            alpha = jnp.exp2(m_prev - m_new)
            p = jnp.exp2(s - m_new[0:1])
            l_ref[c] = alpha * l_ref[c] + jnp.sum(p, axis=0, keepdims=True)
            vt = vt_ref[h * 2 * d:(h + 1) * 2 * d, :]
            acc_ref[c] = acc_ref[c] * alpha[0:1] + _dot(vt, p.astype(BF16))
            m_ref[c] = m_new

    @pl.when(bidx < n_tiles)
    def _():
        sweep(True)

    @pl.when(bidx >= n_tiles)
    def _():
        sweep(False)

    @pl.when(kj == qi // kv_mult)
    def _():
        lam = _diff_lambda(lam_ref, lam_init)
        for h in range(H):
            ot = (acc_ref[2 * h] / l_ref[2 * h][0:1] - lam * (acc_ref[2 * h + 1] / l_ref[2 * h + 1][0:1]))
            _subnorm_store(ot.T, h, sn_ref, o_ref, lam_init)


def _diff_attention_prompt(q, k, vt, rel_bias, lam_vecs, sub_norm, lam_init, *, nseq, d):
    T, QW = q.shape
    H = QW // (2 * d)
    L = T // nseq
    tq = _tile(L, 256, LANES)
    kv_mult = 2 if L % (2 * tq) == 0 else 1
    tk = kv_mult * tq
    assert tq % CHUNK == 0 and L % tq == 0
    nq, nk = L // tq, L // tk
    pairs = [(i, j) for i in range(nq) for j in range(i // kv_mult + 1)]
    qi = jnp.asarray([pq for pq, _ in pairs], jnp.int32)
    kj = jnp.asarray([pk for _, pk in pairs], jnp.int32)
    n_tiles = -(-(tk + MAX_DISTANCE - 1) // tq)
    bias = _bias_tiles(rel_bias, [(t * tq, 0) for t in range(n_tiles)], tq, tk, keys_first=True)
    qrow = lambda b, p, qi_r, kj_r: (b * nq + qi_r[p], 0)
    krow = lambda b, p, qi_r, kj_r: (b * nk + kj_r[p], 0)
    kcol = lambda b, p, qi_r, kj_r: (0, b * nk + kj_r[p])
    fixed2 = lambda b, p, qi_r, kj_r: (0, 0)
    grid_spec = pltpu.PrefetchScalarGridSpec(
        num_scalar_prefetch=2,
        grid=(nseq, len(pairs)),
        in_specs=[
            pl.BlockSpec((tq, QW), qrow),
            pl.BlockSpec((tk, QW), krow),
            pl.BlockSpec((QW, tk), kcol),
            pl.BlockSpec((n_tiles, H, tk, tq), lambda b, p, qi_r, kj_r: (0, 0, 0, 0),
                         pipeline_mode=pl.Buffered(1)),
            pl.BlockSpec((4, d), fixed2),
            pl.BlockSpec((1, 2 * d), fixed2),
        ],
        out_specs=pl.BlockSpec((tq, QW), qrow),
        scratch_shapes=[pltpu.VMEM((2 * H, F32_SUBLANES, tq), F32), pltpu.VMEM((2 * H, F32_SUBLANES, tq), F32),
                        pltpu.VMEM((2 * H, 2 * d, tq), F32)],
    )
    return pl.pallas_call(
        functools.partial(_attn_body, H=H, d=d, lam_init=lam_init, kv_mult=kv_mult, n_tiles=n_tiles),
        grid_spec=grid_spec,
        out_shape=jax.ShapeDtypeStruct((T, QW), BF16),
        compiler_params=_cparams(("parallel", "arbitrary")),
        name="diff_attention_prompt",
    )(qi, kj, q, k, vt, bias, lam_vecs, sub_norm.astype(F32).reshape(1, 2 * d))


def _attn_dec_body(q_ref, ck_ref, cv_ref, kn_ref, vn_ref, bc_ref, bn_ref, lam_ref, sn_ref, o_ref,
                   m_ref, l_ref, acc_ref, *, H, d, lam_init, n_blocks):
    kb_i = pl.program_id(1)

    @pl.when(kb_i == 0)
    def _():
        m_ref[...] = jnp.full(m_ref.shape, -jnp.inf, F32)
        l_ref[...] = jnp.zeros(l_ref.shape, F32)
        acc_ref[...] = jnp.zeros(acc_ref.shape, F32)

    def sweep(k_of, v_of, b_ref):
        for h in range(H):
            vb = v_of(h)
            for mp in range(2):
                c = 2 * h + mp
                s = _dot_nt(q_ref[:, c * d:(c + 1) * d], k_of(c)) + b_ref[h]
                _softmax_update(s, vb, m_ref, l_ref, acc_ref, c)

    tk = ck_ref.shape[1] // (2 * H)
    sweep(lambda c: ck_ref[0, pl.ds(c, tk, stride=2 * H), :].astype(BF16),
          lambda h: jnp.concatenate([cv_ref[0, pl.ds(half * H + h, tk, stride=2 * H), :] for half in range(2)],
                                    axis=1).astype(BF16), bc_ref)

    @pl.when(kb_i == n_blocks - 1)
    def _():
        sweep(lambda c: kn_ref[:, c * d:(c + 1) * d], lambda h: vn_ref[:, h * 2 * d:(h + 1) * 2 * d], bn_ref)
        _diff_finalize(lam_ref, sn_ref, o_ref, l_ref, acc_ref, H=H, d=d, lam_init=lam_init)


def _diff_attention_decode(q, k_new, v_new, cache_k, cache_v, rel_bias, lam_vecs, sub_norm, lam_init, *, d):
    B, P, H = cache_v.shape[:3]
    QW = H * 2 * d
    Lq = q.shape[0] // B
    tk = _tile(P, 512, LANES)
    nb = P // tk
    cache_v_rows = cache_v.reshape(B, P, H, 2, d).transpose(0, 1, 3, 2, 4).reshape(B, P * 2 * H, d)
    bias_c = _bias_tiles(rel_bias, [(P, j * tk) for j in range(nb)], Lq, tk)
    bias_n = _bias_tiles(rel_bias, [(P, P)], Lq, Lq)[0]
    seq = lambda b, j: (b, 0)
    fixed2 = lambda b, j: (0, 0)
    return pl.pallas_call(
        functools.partial(_attn_dec_body, H=H, d=d, lam_init=lam_init, n_blocks=nb),
        grid=(B, nb),
        in_specs=[
            pl.BlockSpec((Lq, QW), seq),
            pl.BlockSpec((1, tk * 2 * H, d), lambda b, j: (b, j, 0)),
            pl.BlockSpec((1, tk * 2 * H, d), lambda b, j: (b, j, 0)),
            pl.BlockSpec((Lq, QW), seq),
            pl.BlockSpec((Lq, QW), seq),
            pl.BlockSpec((None, H, Lq, tk), lambda b, j: (j, 0, 0, 0)),
            pl.BlockSpec((H, Lq, Lq), lambda b, j: (0, 0, 0)),
            pl.BlockSpec((4, d), fixed2),
            pl.BlockSpec((1, 2 * d), fixed2),
        ],
        out_specs=pl.BlockSpec((Lq, QW), seq),
        out_shape=jax.ShapeDtypeStruct((B * Lq, QW), BF16),
        scratch_shapes=[pltpu.VMEM((2 * H, Lq, LANES), F32), pltpu.VMEM((2 * H, Lq, LANES), F32),
                        pltpu.VMEM((2 * H, Lq, 2 * d), F32)],
        compiler_params=_cparams(("parallel", "arbitrary")),
        name="diff_attention_decode",
    )(q, cache_k.reshape(B, P * 2 * H, d), cache_v_rows, k_new, v_new, bias_c, bias_n,
      lam_vecs, sub_norm.astype(F32).reshape(1, 2 * d))


def _trunk(x, p, st_gdn, st_gdn_conv, st_ffn_conv, cache_k, cache_v):
    B, L, D = x.shape
    fresh = cache_k is None
    T = B * L
    depth = p['f_norm'].shape[0]
    n_a = p['a_norm'].shape[0]
    HV, dv = p['a_log'].shape[1], p['a_out_norm'].shape[1]
    VW = HV * dv
    QKVW = p['a_w_conv'].shape[2]
    dk = p['gdn_head_k']
    d = p['b_lam_q1'].shape[1]
    DFF = p['f_w_down'].shape[1]
    QW = p['b_w_q'].shape[2]

    time_major = not fresh
    if time_major:
        h = x.transpose(1, 0, 2).reshape(T, D)
        stride, nseq_conv = B, 1
    else:
        h = x.reshape(T, D)
        stride, nseq_conv = 1, B

    def to_seq(t):
        return t.reshape(L, B, -1).transpose(1, 0, 2).reshape(T, -1) if time_major else t

    def to_time(t):
        return t.reshape(B, L, -1).transpose(1, 0, 2).reshape(T, -1) if time_major else t

    def conv_state_in(st, width, chans):
        if st is None:
            return jnp.zeros((nseq_conv, (width - 1) * stride, chans), F32)
        return st.transpose(1, 0, 2).reshape(1, (width - 1) * B, chans)

    def conv_state_out(st, width):
        if time_major:
            return st.reshape(width - 1, B, -1).transpose(1, 0, 2)
        return st

    new_S, new_gconv, new_fconv = [], [], []
    k_f32 = v_f32 = k_bf = v_bf = None
    xn = _rmsnorm_cast(h, p['a_norm'][0]) if n_a > 0 else _rmsnorm_cast(h, p['b_norm'][0])
    xn_kv = None
    y = None
    for layer in range(depth):
        if layer < n_a:
            i = layer
            w_in = (i, p['a_w_in_bf'])
            w_ba = p['a_w_in_bf'][i][:, QKVW + VW:]
            gw = p['a_w_conv'].shape[1]
            cst = conv_state_in(None if fresh else st_gdn_conv[i], gw, QKVW)
            qkv, gconv = _conv_matmul(xn, w_in, p['a_w_conv'][i], None, cst, groups=1, width=QKVW,
                                      stride=stride, nseq=nseq_conv, glu=False, name="gdn_qkv_conv",
                                      tm_pref=512, tn_pref=4096)
            (sz,) = _matmul(xn, w_in, lambda acc: (_silu(acc),), [BF16], "gdn_z_proj", col0=QKVW, width=VW)
            beta, g = _gate_proj(xn, w_ba[:, :HV], w_ba[:, HV:], p['a_log'][i], p['a_dt_bias'][i],
                                 stride=stride, chunk=min(CHUNK, L))
            S0 = jnp.zeros((B, HV, dk, dv), F32) if fresh else st_gdn[i]
            og, S = _gdn_core(to_seq(qkv), to_seq(sz), to_seq(beta), to_seq(g), S0, p['a_out_norm'][i],
                              nseq=B, chunk=min(CHUNK, L), hb=min(GDN_HEADS_PER_STEP, HV))
            new_S.append(S)
            new_gconv.append(conv_state_out(gconv, gw))
            h, xn = _matmul_residual(to_time(og), (i, p['a_w_out_bf']), h, [p['f_norm'][layer]],
                                     BF16, True, "gdn_out_proj")
        else:
            j = layer - n_a
            lam_init = 0.8 - 0.6 * math.exp(-0.3 * layer)
            scale = d ** -0.5 * LOG2_E
            (q,) = _matmul(xn, (j, p['b_w_q_bf']), lambda acc: (acc * scale,), [BF16], "diff_q_proj")
            lam_vecs = jnp.stack([p['b_lam_q1'][j], p['b_lam_k1'][j], p['b_lam_q2'][j],
                                  p['b_lam_k2'][j]]).astype(F32)
            if fresh:
                ao = _diff_attention_prompt(q, k_bf, v_bf, p['rel_bias'], lam_vecs, p['b_sub_norm'][j],
                                            lam_init, nseq=B, d=d)
            else:
                ao = _diff_attention_decode(to_seq(q), to_seq(k_bf), to_seq(v_bf),
                                            cache_k, cache_v, p['rel_bias'], lam_vecs, p['b_sub_norm'][j], lam_init, d=d)
                ao = to_time(ao)
            h, xn = _matmul_residual(ao, (j, p['b_w_o_bf']), h, [p['f_norm'][layer]],
                                     BF16, True, "diff_out_proj")
        fw = p['f_w_conv'].shape[1]
        fst = conv_state_in(None if fresh else st_ffn_conv[layer], fw, 2 * DFF)
        act, fconv = _conv_matmul(xn, (layer, p['f_w_up_bf']), p['f_w_conv'][layer],
                                  p['f_b_conv'][layer], fst, groups=2, width=2 * DFF, stride=stride,
                                  nseq=nseq_conv, glu=True, name="ffn_up_conv", tm_pref=512, tn_pref=2816)
        new_fconv.append(conv_state_out(fconv, fw))
        last = layer == depth - 1
        gains = []
        if layer == n_a - 1:
            gains.append(p['kv_norm'])
        if last:
            gains.append(p['final_norm'])
        elif layer + 1 < n_a:
            gains.append(p['a_norm'][layer + 1])
        else:
            gains.append(p['b_norm'][layer + 1 - n_a])
        outs = _matmul_residual(act, (layer, p['f_w_down_bf']), h, gains,
                                F32 if last else BF16, not last, "ffn_down_proj")
        if last:
            if layer == n_a - 1:
                xn_kv = outs[0].astype(BF16)
            y = outs[-1]
        else:
            h = outs[0]
            if layer == n_a - 1:
                xn_kv = outs[1]
            xn = outs[-1]
        if layer == n_a - 1:
            k_f32, k_bf = _head_projection(xn_kv, p['w_kv_bf'], 0, QW, d, "k_proj")
            v_f32, v_bf = _head_projection(xn_kv, p['w_kv_bf'], QW, QW, 2 * d, "v_proj", features_first=fresh)

    n_kh = 2 * (QW // (2 * d))
    y = to_seq(y).reshape(B, L, D)
    k_sh = to_seq(k_f32).reshape(B, L, n_kh, d)
    v_sh = to_seq(v_f32).reshape(B, L, n_kh // 2, 2 * d)
    return y, jnp.stack(new_S), jnp.stack(new_gconv), jnp.stack(new_fconv), k_sh, v_sh


def kernel(x_prompt, x_sample, state_gdn, state_gdn_conv, state_ffn_conv, cache_k, cache_v, a_norm, a_w_in, a_w_conv, a_log, a_dt_bias, a_out_norm, a_w_out, kv_norm, w_kv, b_norm, b_w_q, b_lam_q1, b_lam_k1, b_lam_q2, b_lam_k2, b_sub_norm, b_w_o, rel_bias, f_norm, f_w_up, f_w_conv, f_b_conv, f_w_down, final_norm):
    p = {
        'a_norm': a_norm, 'a_w_in': a_w_in, 'a_w_conv': a_w_conv, 'a_log': a_log,
        'a_dt_bias': a_dt_bias, 'a_out_norm': a_out_norm, 'a_w_out': a_w_out,
        'kv_norm': kv_norm, 'w_kv': w_kv,
        'b_norm': b_norm, 'b_w_q': b_w_q, 'b_lam_q1': b_lam_q1, 'b_lam_k1': b_lam_k1,
        'b_lam_q2': b_lam_q2, 'b_lam_k2': b_lam_k2, 'b_sub_norm': b_sub_norm, 'b_w_o': b_w_o,
        'rel_bias': rel_bias,
        'f_norm': f_norm, 'f_w_up': f_w_up, 'f_w_conv': f_w_conv, 'f_b_conv': f_b_conv,
        'f_w_down': f_w_down, 'final_norm': final_norm,
        'gdn_head_k': state_gdn.shape[3],
    }
    for name in ('a_w_in', 'a_w_out', 'w_kv', 'b_w_q', 'b_w_o', 'f_w_up', 'f_w_down'):
        p[name + '_bf'] = p[name].astype(BF16)
    out_p = _trunk(x_prompt, p, None, None, None, None, None)
    out_s = _trunk(x_sample, p, state_gdn, state_gdn_conv, state_ffn_conv, cache_k, cache_v)
    return (out_p[0], out_s[0]) + tuple(out_p[1:]) + tuple(out_s[1:])
```

```python
import functools
import math

import numpy as np
import jax
import jax.numpy as jnp
from jax import lax
from jax.experimental import pallas as pl
from jax.experimental.pallas import tpu as pltpu

F32 = jnp.float32
BF16 = jnp.bfloat16

CHUNK = 64
NORM_EPS = 1e-6
DIFF_SUBLN_EPS = 1e-5
L2_EPS = 1e-6
MAX_DISTANCE = 128
NEG_INF = -1e30
LOG2_E = math.log2(math.e)
GDN_HEADS_PER_STEP = 32

V7X_VMEM_LIMIT = 56 * 1024 * 1024
RESIDUAL_MATMUL_VMEM_BUDGET = 46 * 1024 * 1024
LANES = 128
V7X_MXU_COLS = 256
F32_SUBLANES = 8
BF16_SUBLANES = 16


def _cparams(sem):
    return pltpu.CompilerParams(dimension_semantics=sem, vmem_limit_bytes=V7X_VMEM_LIMIT)


def _tile(n, pref, mult):
    if n <= pref:
        return n
    t = (pref // mult) * mult
    while t >= mult:
        if n % t == 0:
            return t
        t -= mult
    return n


def _wspec(w, block, index_map, **kw):
    if isinstance(w, tuple):
        layer = w[0]
        return pl.BlockSpec((None,) + block, lambda *a: (layer,) + index_map(*a), **kw)
    return pl.BlockSpec(block, index_map, **kw)


def _warr(w):
    return w[1] if isinstance(w, tuple) else w


def _dot(a, b):
    return jnp.dot(a, b, preferred_element_type=F32)


def _dot_nt(a, b):
    return lax.dot_general(a, b, (((1,), (1,)), ((), ())), preferred_element_type=F32)


def _dot_tn(a, b):
    return lax.dot_general(a, b, (((0,), (0,)), ((), ())), preferred_element_type=F32)


def _dot_hi(a, b):
    return jnp.dot(a, b, preferred_element_type=F32, precision=lax.Precision.HIGHEST)


def _split_bf16(x):
    bits = lax.bitcast_convert_type(x, jnp.uint32) & jnp.uint32(0xFFFF0000)
    hi = lax.bitcast_convert_type(bits, F32)
    return hi, x - hi


def _x4_lhs(x, parts=None):
    xh, xl = parts or _split_bf16(x)
    xx = jnp.concatenate([xh, xl], axis=1).astype(BF16)
    return jnp.concatenate([xx, xx], axis=1)


def _x4_rhs(y, parts=None):
    yh, yl = parts or _split_bf16(y)
    return jnp.concatenate([yh, yh, yl, yl], axis=0).astype(BF16)


def _x4_both(x):
    parts = _split_bf16(x)
    return _x4_lhs(x, parts), _x4_rhs(x, parts)


def _dot_x4(lhs4, rhs4):
    return _dot(lhs4, rhs4)


def _silu(x):
    h = 0.5 * x
    return h + h * jnp.tanh(h)


def _rms_body(x_ref, g_ref, o_ref):
    x = x_ref[...]
    ms = jnp.mean(x * x, axis=-1, keepdims=True)
    o_ref[...] = (x * lax.rsqrt(ms + NORM_EPS) * g_ref[...]).astype(o_ref.dtype)


def _rmsnorm_cast(x, g):
    T, D = x.shape
    tm = _tile(T, 512, BF16_SUBLANES)
    return pl.pallas_call(
        _rms_body,
        grid=(T // tm,),
        in_specs=[pl.BlockSpec((tm, D), lambda i: (i, 0)), pl.BlockSpec((1, D), lambda i: (0, 0))],
        out_specs=pl.BlockSpec((tm, D), lambda i: (i, 0)),
        out_shape=jax.ShapeDtypeStruct((T, D), BF16),
        compiler_params=_cparams(("parallel",)),
        name="rmsnorm_cast",
    )(x, g.reshape(1, D).astype(F32))


def _mm_body(x_ref, w_ref, *o_refs, epilogue):
    acc = _dot(x_ref[...], w_ref[...])
    for o_ref, val in zip(o_refs, epilogue(acc)):
        o_ref[...] = val.astype(o_ref.dtype)


def _matmul(x, w, epilogue, out_dtypes, name, col0=0, width=None, tm_pref=1024, tn_pref=1024):
    T, K = x.shape
    N = _warr(w).shape[-1] if width is None else width
    tm = _tile(T, tm_pref, BF16_SUBLANES)
    tn = _tile(math.gcd(N, col0) if col0 else N, tn_pref, LANES)
    cb = col0 // tn
    outs = pl.pallas_call(
        functools.partial(_mm_body, epilogue=epilogue),
        grid=(N // tn, T // tm),
        in_specs=[pl.BlockSpec((tm, K), lambda n, m: (m, 0)), _wspec(w, (K, tn), lambda n, m: (0, n + cb))],
        out_specs=[pl.BlockSpec((tm, tn), lambda n, m: (m, n)) for _ in out_dtypes],
        out_shape=[jax.ShapeDtypeStruct((T, N), dt) for dt in out_dtypes],
        compiler_params=_cparams(("parallel", "parallel")),
        name=name,
    )(x, _warr(w))
    return outs


def _headproj_body(x_ref, w_ref, o3_ref, obf_ref, *, hd, features_first):
    acc = _dot(x_ref[...], w_ref[...])
    obf_ref[...] = (acc.T if features_first else acc).astype(obf_ref.dtype)
    for hh in range(o3_ref.shape[1]):
        o3_ref[:, hh, :] = acc[:, hh * hd:(hh + 1) * hd]


def _head_projection(x, w, col0, width, hd, name, features_first=False):
    T, K = x.shape
    nh = width // hd
    hb = nh if nh <= F32_SUBLANES else F32_SUBLANES
    tn = hb * hd
    tm = _tile(T, 512, LANES)
    cb = col0 // tn
    if features_first:
        bf_spec = pl.BlockSpec((tn, tm), lambda n, m: (n, m))
        bf_shape = jax.ShapeDtypeStruct((width, T), BF16)
    else:
        bf_spec = pl.BlockSpec((tm, tn), lambda n, m: (m, n))
        bf_shape = jax.ShapeDtypeStruct((T, width), BF16)
    return pl.pallas_call(
        functools.partial(_headproj_body, hd=hd, features_first=features_first),
        grid=(width // tn, T // tm),
        in_specs=[pl.BlockSpec((tm, K), lambda n, m: (m, 0)), pl.BlockSpec((K, tn), lambda n, m: (0, n + cb))],
        out_specs=[pl.BlockSpec((tm, hb, hd), lambda n, m: (m, n, 0)), bf_spec],
        out_shape=[jax.ShapeDtypeStruct((T, nh, hd), F32), bf_shape],
        compiler_params=_cparams(("parallel", "parallel")),
        name=name,
    )(x, w)


def _gate_body(x_ref, wb_ref, wa_ref, alog_ref, dtb_ref, beta_ref, gcum_ref, *, stride, chunk):
    x = x_ref[...]
    b = _dot(x, wb_ref[...])
    a = _dot(x, wa_ref[...]) + dtb_ref[...]
    beta_ref[...] = 1.0 / (1.0 + jnp.exp(-b))
    softplus = jnp.maximum(a, 0.0) + jnp.log(1.0 + jnp.exp(-jnp.abs(a)))
    g = -jnp.exp(alog_ref[...]) * softplus
    bs = stride * chunk
    ri = lax.broadcasted_iota(jnp.int32, (bs, bs), 0)
    ci = lax.broadcasted_iota(jnp.int32, (bs, bs), 1)
    earlier = jnp.logical_and(ci <= ri, (ri - ci) % stride == 0).astype(F32)
    for r0 in range(0, x.shape[0], bs):
        gcum_ref[r0:r0 + bs, :] = _dot_hi(earlier, g[r0:r0 + bs])


def _gate_proj(xn, w_b, w_a, a_log, dt_bias, *, stride, chunk):
    T, K = xn.shape
    H = w_b.shape[1]
    pad = LANES - H
    wb = jnp.pad(w_b, ((0, 0), (0, pad))).astype(BF16)
    wa = jnp.pad(w_a, ((0, 0), (0, pad))).astype(BF16)
    al = jnp.pad(a_log.astype(F32), (0, pad)).reshape(1, LANES)
    db = jnp.pad(dt_bias.astype(F32), (0, pad)).reshape(1, LANES)
    tm = _tile(T, 1024, stride * chunk)
    row = pl.BlockSpec((tm, K), lambda i: (i, 0))
    wsp = pl.BlockSpec((K, LANES), lambda i: (0, 0))
    vsp = pl.BlockSpec((1, LANES), lambda i: (0, 0))
    osp = pl.BlockSpec((tm, LANES), lambda i: (i, 0))
    beta, gcum = pl.pallas_call(
        functools.partial(_gate_body, stride=stride, chunk=chunk),
        grid=(T // tm,),
        in_specs=[row, wsp, wsp, vsp, vsp],
        out_specs=[osp, osp],
        out_shape=[jax.ShapeDtypeStruct((T, LANES), F32)] * 2,
        compiler_params=_cparams(("parallel",)),
        name="gdn_gate_proj",
    )(xn, wb, wa, al, db)
    return beta[:, :H], gcum[:, :H]


def _convmm_body(*refs, G, W, stride, tm, pad, tps, glu, n_sub):
    hist = (W - 1) * stride
    it = iter(refs)
    x_ref = next(it)
    w_refs = [next(it) for _ in range(G)]
    wc_refs = [next(it) for _ in range(G)]
    b_refs = [next(it) for _ in range(G)] if glu else None
    st_refs = [next(it) for _ in range(G)]
    out_ref = next(it)
    nst_refs = [next(it) for _ in range(G)]
    yscs = [[next(it) for _ in range(n_sub)] for _ in range(G)]

    m = pl.program_id(1)
    first = (m % tps) == 0
    last = (m % tps) == tps - 1
    sub = out_ref.shape[1] // n_sub

    @pl.when(first)
    def _():
        for g in range(G):
            for c in range(n_sub):
                yscs[g][c][pad - hist:pad, :] = st_refs[g][:, c * sub:(c + 1) * sub]

    @pl.when(jnp.logical_not(first))
    def _():
        for g in range(G):
            for c in range(n_sub):
                yscs[g][c][0:pad, :] = yscs[g][c][tm:tm + pad, :]

    def project(c):
        cols = slice(c * sub, (c + 1) * sub)
        for g in range(G):
            yscs[g][c][pad:pad + tm, :] = _dot(x_ref[...], w_refs[g][:, cols])

    def conv_act(c):
        cols = slice(c * sub, (c + 1) * sub)
        convs = []
        for g in range(G):
            acc = None
            for i in range(W):
                off = pad - (W - 1 - i) * stride
                term = yscs[g][c][off:off + tm, :] * wc_refs[g][i:i + 1, cols]
                acc = term if acc is None else acc + term
            if glu:
                acc = acc + b_refs[g][:, cols]
            convs.append(acc)
        res = _silu(convs[0]) * convs[1] if glu else _silu(convs[0])
        out_ref[:, cols] = res.astype(out_ref.dtype)

    project(0)
    for c in range(1, n_sub):
        project(c)
        conv_act(c - 1)
    conv_act(n_sub - 1)

    @pl.when(last)
    def _():
        for g in range(G):
            for c in range(n_sub):
                nst_refs[g][:, c * sub:(c + 1) * sub] = yscs[g][c][pad + tm - hist:pad + tm, :]


def _conv_matmul(x, w, wc, bias, state, *, groups, width, stride, nseq, glu, name, tm_pref, tn_pref):
    T, K = x.shape
    W = wc.shape[0]
    Ng = width // groups
    hist = (W - 1) * stride
    rows = T // nseq
    tm = _tile(rows, tm_pref, BF16_SUBLANES)
    stream_weights = T == tm
    tn = _tile(Ng, 2 * V7X_MXU_COLS if stream_weights else tn_pref, V7X_MXU_COLS)
    n_sub = max(tn // V7X_MXU_COLS, 1)
    tps = rows // tm
    pad = -(-hist // F32_SUBLANES) * F32_SUBLANES
    assert tm >= pad and state.shape == (nseq, hist, groups * Ng)
    nb = Ng // tn

    def col(g):
        return lambda n, m: (0, n + g * nb)

    def stcol(g):
        return lambda n, m: (m // tps, 0, n + g * nb)

    in_specs = [pl.BlockSpec((tm, K), lambda n, m: (m, 0))]
    args = [x]
    wmode = {} if stream_weights else {'pipeline_mode': pl.Buffered(1)}
    in_specs += [_wspec(w, (K, tn), col(g), **wmode) for g in range(groups)]
    args += [_warr(w)] * groups
    in_specs += [pl.BlockSpec((W, tn), col(g)) for g in range(groups)]
    args += [wc.astype(F32)] * groups
    if glu:
        in_specs += [pl.BlockSpec((1, tn), col(g)) for g in range(groups)]
        args += [bias.astype(F32).reshape(1, -1)] * groups
    in_specs += [pl.BlockSpec((None, hist, tn), stcol(g)) for g in range(groups)]
    args += [state.astype(F32)] * groups
    out_specs = [pl.BlockSpec((tm, tn), lambda n, m: (m, n))]
    out_specs += [pl.BlockSpec((None, hist, tn), stcol(0)) for g in range(groups)]
    out_shape = [jax.ShapeDtypeStruct((T, Ng), BF16)]
    out_shape += [jax.ShapeDtypeStruct((nseq, hist, Ng), F32) for g in range(groups)]
    outs = pl.pallas_call(
        functools.partial(_convmm_body, G=groups, W=W, stride=stride, tm=tm, pad=pad, tps=tps, glu=glu,
                          n_sub=n_sub),
        grid=(nb, T // tm),
        in_specs=in_specs,
        out_specs=out_specs,
        out_shape=out_shape,
        scratch_shapes=[pltpu.VMEM((pad + tm, tn // n_sub), F32) for _ in range(groups * n_sub)],
        compiler_params=_cparams(("parallel", "arbitrary")),
        name=name,
    )(*args)
    new_state = outs[1] if groups == 1 else jnp.concatenate(outs[1:], axis=-1)
    return outs[0], new_state


def _gdn_body(q_ref, k_ref, v_ref, z_ref, beta_ref, g_ref, s0_ref, gn_ref, o_ref, sout_ref, S,
              *, hb, rep, dk, dv, n_blocks, C):
    n = pl.program_id(2)

    @pl.when(n == 0)
    def _():
        S[...] = s0_ref[0]

    ri = lax.broadcasted_iota(jnp.int32, (C, C), 0)
    ci = lax.broadcasted_iota(jnp.int32, (C, C), 1)
    incl = ri >= ci
    strict = ri > ci
    eye = (ri == ci).astype(F32)
    gain = gn_ref[...]
    n_sq = int(round(math.log2(C))) - 1
    heads = range(hb)

    def prepare(r0):
        rows = slice(r0, r0 + C)
        Gc = g_ref[rows, :]
        beta = beta_ref[rows, :]
        kn, qn, kk, qk = [], [], [], []
        for jk in range(hb // rep):
            kf = k_ref[rows, jk * dk:(jk + 1) * dk].astype(F32)
            qf = q_ref[rows, jk * dk:(jk + 1) * dk].astype(F32)
            kn.append(kf * lax.rsqrt(jnp.sum(kf * kf, axis=-1, keepdims=True) + L2_EPS))
            qn.append(qf * lax.rsqrt(jnp.sum(qf * qf, axis=-1, keepdims=True) + L2_EPS) * (dk ** -0.5))
            kb = kn[jk].astype(BF16)
            kq = _dot_nt(jnp.concatenate([kb, qn[jk].astype(BF16)], axis=0), kb)
            kk.append(kq[:C])
            qk.append(kq[C:])
        Gcol = [Gc[:, j:j + 1] for j in heads]
        bcol = [beta[:, j:j + 1] for j in heads]
        dec_incl, Nm = [], []
        for j in heads:
            Grow = jnp.sum(eye * Gcol[j], axis=0, keepdims=True)
            diff = Gcol[j] - Grow
            dec = jnp.where(incl, jnp.exp(jnp.where(incl, diff, 0.0)), 0.0)
            dec_incl.append(dec)
            Nm.append(-(bcol[j] * kk[j // rep] * jnp.where(strict, dec, 0.0)))
        Tm = [eye + Nm[j] for j in heads]
        Np = [_dot_x4(*_x4_both(Nm[j])) for j in heads]
        for lvl in range(n_sq):
            nparts = [_split_bf16(Np[j]) for j in heads]
            rhs4 = [_x4_rhs(None, nparts[j]) for j in heads]
            tparts = [_split_bf16(Tm[j]) for j in heads]
            if lvl + 1 < n_sq:
                stacked = [tuple(jnp.concatenate([tp, npart], axis=0) for tp, npart in zip(tparts[j], nparts[j]))
                           for j in heads]
                res = [_dot_x4(_x4_lhs(None, stacked[j]), rhs4[j]) for j in heads]
                Tm = [Tm[j] + res[j][:C] for j in heads]
                Np = [res[j][C:] for j in heads]
            else:
                Tm = [Tm[j] + _dot_x4(_x4_lhs(None, tparts[j]), rhs4[j]) for j in heads]
        eG = [jnp.exp(Gcol[j]) for j in heads]
        sol = []
        for j in heads:
            vf = v_ref[rows, j * dv:(j + 1) * dv].astype(F32)
            rhs = jnp.concatenate([bcol[j] * vf, (bcol[j] * eG[j]) * kn[j // rep]], axis=1)
            sol.append(_dot_x4(_x4_lhs(Tm[j]), _x4_rhs(rhs)))
        wq, pk, u_v, decay = [], [], [], []
        for j in heads:
            Glast = Gc[C - 1:C, j:j + 1]
            q_g = qn[j // rep] * eG[j]
            wq.append(jnp.concatenate([sol[j][:, dv:].astype(BF16), q_g.astype(BF16)], axis=0))
            P = qk[j // rep] * dec_incl[j]
            k_d = kn[j // rep] * jnp.exp(Glast - Gcol[j])
            pk.append(jnp.concatenate([P.astype(BF16), k_d.T.astype(BF16)], axis=0))
            u_v.append(sol[j][:, :dv])
            decay.append(jnp.exp(Glast))
        return wq, pk, u_v, decay

    n_sub = q_ref.shape[0] // C
    prepared = [prepare(cc * C) for cc in range(n_sub)]
    Scur = [S[j] for j in heads]
    for cc in range(n_sub):
        rows = slice(cc * C, (cc + 1) * C)
        wq, pk, u_v, decay = prepared[cc]
        Sb = [Scur[j].astype(BF16) for j in heads]
        wqS = [_dot(wq[j], Sb[j]) for j in heads]
        ub = [(u_v[j] - wqS[j][:C]).astype(BF16) for j in heads]
        pku = [_dot(pk[j], ub[j]) for j in heads]
        Scur = [decay[j] * Scur[j] + pku[j][C:] for j in heads]
        for j in heads:
            o = wqS[j][C:] + pku[j][:C]
            on = o * lax.rsqrt(jnp.mean(o * o, axis=-1, keepdims=True) + NORM_EPS) * gain
            o_ref[rows, j * dv:(j + 1) * dv] = (
                on * z_ref[rows, j * dv:(j + 1) * dv].astype(F32)).astype(o_ref.dtype)
    for j in heads:
        S[j] = Scur[j]

    @pl.when(n == n_blocks - 1)
    def _():
        sout_ref[0] = S[...]


def _gdn_core(qkv, sz, beta, g, S0, out_norm, *, nseq, chunk, hb):
    T = qkv.shape[0]
    _, HV, dk, dv = S0.shape
    VW = HV * dv
    QK = (qkv.shape[1] - VW) // 2
    HK = QK // dk
    rep = HV // HK
    HG = HV // hb
    kb = hb // rep
    L = T // nseq
    n_chunks = L // chunk
    n_blocks, rows = n_chunks, chunk

    def group_major(t):
        t = t.reshape(T, HG, hb).transpose(1, 0, 2)
        return jnp.pad(t, ((0, 0), (0, 0), (0, LANES - hb)))

    rowblk = lambda b, h, n: b * n_blocks + n
    in_specs = [
        pl.BlockSpec((rows, kb * dk), lambda b, h, n: (rowblk(b, h, n), h)),
        pl.BlockSpec((rows, kb * dk), lambda b, h, n: (rowblk(b, h, n), HG + h)),
        pl.BlockSpec((rows, hb * dv), lambda b, h, n: (rowblk(b, h, n), 2 * QK // (hb * dv) + h)),
        pl.BlockSpec((rows, hb * dv), lambda b, h, n: (rowblk(b, h, n), h)),
        pl.BlockSpec((None, rows, LANES), lambda b, h, n: (h, rowblk(b, h, n), 0)),
        pl.BlockSpec((None, rows, LANES), lambda b, h, n: (h, rowblk(b, h, n), 0)),
        pl.BlockSpec((1, hb, dk, dv), lambda b, h, n: (b, h, 0, 0)),
        pl.BlockSpec((1, dv), lambda b, h, n: (0, 0)),
    ]
    out_specs = [
        pl.BlockSpec((rows, hb * dv), lambda b, h, n: (rowblk(b, h, n), h)),
        pl.BlockSpec((1, hb, dk, dv), lambda b, h, n: (b, h, 0, 0)),
    ]
    o, S = pl.pallas_call(
        functools.partial(_gdn_body, hb=hb, rep=rep, dk=dk, dv=dv, n_blocks=n_blocks, C=chunk),
        grid=(nseq, HG, n_blocks),
        in_specs=in_specs,
        out_specs=out_specs,
        out_shape=[jax.ShapeDtypeStruct((T, VW), BF16), jax.ShapeDtypeStruct(S0.shape, F32)],
        scratch_shapes=[pltpu.VMEM((hb, dk, dv), F32)],
        compiler_params=_cparams(("parallel", "parallel", "arbitrary")),
        name="gdn_core",
    )(qkv, qkv, qkv, sz, group_major(beta), group_major(g), S0.astype(F32),
      out_norm.astype(F32).reshape(1, dv))
    return o, S


def _mmres_body(x_ref, w_ref, res_ref, *refs, n_norm, with_h):
    gains = refs[:n_norm]
    outs = refs[n_norm:]
    h = res_ref[...] + _dot(x_ref[...], w_ref[...])
    k = 0
    if with_h:
        outs[0][...] = h
        k = 1
    if n_norm:
        inv = lax.rsqrt(jnp.mean(h * h, axis=-1, keepdims=True) + NORM_EPS)
        y = h * inv
        for i in range(n_norm):
            outs[k + i][...] = (y * gains[i][...]).astype(outs[k + i].dtype)


def _matmul_residual(x, w, res, gains, norm_dtype, with_h, name):
    T, K = x.shape
    D = _warr(w).shape[-1]
    n_f32 = (1 if with_h else 0) + (len(gains) if norm_dtype == F32 else 0)
    n_bf16 = len(gains) if norm_dtype == BF16 else 0

    def vmem_bytes(rows):
        return K * D * 2 + 2 * rows * (K * 2 + D * 4 + n_f32 * D * 4 + n_bf16 * D * 2)

    tm = next(t for t in (1024, 512, 256, 128, BF16_SUBLANES)
              if t == BF16_SUBLANES or vmem_bytes(t) <= RESIDUAL_MATMUL_VMEM_BUDGET)
    tm = _tile(T, tm, BF16_SUBLANES)
    row = lambda i: (i, 0)
    fixed = lambda i: (0, 0)
    n_norm = len(gains)
    in_specs = [pl.BlockSpec((tm, K), row),
                _wspec(w, (K, D), fixed, pipeline_mode=pl.Buffered(1)),
                pl.BlockSpec((tm, D), row)]
    in_specs += [pl.BlockSpec((1, D), fixed) for _ in gains]
    out_shape = ([jax.ShapeDtypeStruct((T, D), F32)] if with_h else []) \
        + [jax.ShapeDtypeStruct((T, D), norm_dtype) for _ in gains]
    outs = pl.pallas_call(
        functools.partial(_mmres_body, n_norm=n_norm, with_h=with_h),
        grid=(T // tm,),
        in_specs=in_specs,
        out_specs=[pl.BlockSpec((tm, D), row) for _ in out_shape],
        out_shape=out_shape,
        compiler_params=_cparams(("parallel",)),
        name=name,
    )(x, _warr(w), res, *[g.astype(F32).reshape(1, D) for g in gains])
    return outs


def _bucket_table(num_buckets):
    half = num_buckets // 2
    exact = half // 2
    n = np.arange(MAX_DISTANCE + 1)
    large = exact + (np.log(np.maximum(n, 1).astype(np.float32) / np.float32(exact))
                     / np.float32(math.log(MAX_DISTANCE / exact)) * np.float32(half - exact)).astype(np.int32)
    large = np.minimum(large, half - 1)
    return np.where(n < exact, n, large), half


def _bias_body(rb_ref, out_ref, *, tiles, steps, half, far_bucket, keys_first):
    h = pl.program_id(0)
    shape = out_ref.shape[2:]
    q_axis, k_axis = (1, 0) if keys_first else (0, 1)
    for t, (q0, k0) in enumerate(tiles):
        qpos = q0 + lax.broadcasted_iota(jnp.int32, shape, q_axis)
        kpos = k0 + lax.broadcasted_iota(jnp.int32, shape, k_axis)
        rel = kpos - qpos
        n = jnp.abs(rel)
        neg = jnp.full(shape, rb_ref[0, h], F32)
        pos = jnp.full(shape, rb_ref[half, h], F32)
        for thr, b in steps:
            ge = n >= thr
            neg = jnp.where(ge, rb_ref[b, h], neg)
            pos = jnp.where(ge, rb_ref[half + b, h], pos)
        bias = jnp.where(rel > 0, pos, neg) - rb_ref[far_bucket, h]
        visible = kpos // CHUNK <= qpos // CHUNK
        out_ref[t, 0] = jnp.where(visible, bias * LOG2_E, NEG_INF)


def _bias_tiles(rel_bias, tiles, nq, nk, keys_first=False):
    NB, H = rel_bias.shape
    table, half = _bucket_table(NB)
    steps = [(int(i), int(table[i])) for i in range(1, len(table)) if table[i] != table[i - 1]]
    shape = (nk, nq) if keys_first else (nq, nk)
    return pl.pallas_call(
        functools.partial(_bias_body, tiles=tuple(tiles), steps=tuple(steps), half=half,
                          far_bucket=int(table[-1]), keys_first=keys_first),
        grid=(H,),
        in_specs=[pl.BlockSpec(memory_space=pltpu.SMEM)],
        out_specs=pl.BlockSpec((len(tiles), 1) + shape, lambda h: (0, h, 0, 0)),
        out_shape=jax.ShapeDtypeStruct((len(tiles), H) + shape, F32),
        compiler_params=_cparams(("parallel",)),
        name="t5_bias_tiles",
    )(rel_bias.astype(F32))


def _lane_tile(x, n):
    if n <= LANES:
        return x[:, :n]
    return jnp.concatenate([x] * (n // LANES), axis=1)


def _softmax_update(s, vb, m_ref, l_ref, acc_ref, idx):
    m_prev = m_ref[idx]
    l_prev = l_ref[idx]
    m_new = jnp.maximum(m_prev, jnp.max(s, axis=1, keepdims=True))
    alpha = jnp.exp2(m_prev - m_new)
    p = jnp.exp2(s - _lane_tile(m_new, s.shape[1]))
    l_ref[idx] = alpha * l_prev + jnp.sum(p, axis=1, keepdims=True)
    acc_ref[idx] = acc_ref[idx] * _lane_tile(alpha, vb.shape[1]) + _dot(p.astype(BF16), vb)
    m_ref[idx] = m_new


def _diff_lambda(lam_ref, lam_init):
    lv = lam_ref[...]
    return (jnp.exp(jnp.sum(lv[0:1] * lv[1:2], axis=1, keepdims=True))
            - jnp.exp(jnp.sum(lv[2:3] * lv[3:4], axis=1, keepdims=True)) + lam_init)


def _subnorm_store(o, h, sn_ref, o_ref, lam_init):
    w = o.shape[1]
    on = o * lax.rsqrt(jnp.mean(o * o, axis=-1, keepdims=True) + DIFF_SUBLN_EPS) * sn_ref[...]
    o_ref[:, h * w:(h + 1) * w] = (on * (1.0 - lam_init)).astype(o_ref.dtype)


def _diff_finalize(lam_ref, sn_ref, o_ref, l_ref, acc_ref, *, H, d, lam_init):
    lam = _diff_lambda(lam_ref, lam_init)
    for h in range(H):
        o = (acc_ref[2 * h] / _lane_tile(l_ref[2 * h], 2 * d)
             - lam * (acc_ref[2 * h + 1] / _lane_tile(l_ref[2 * h + 1], 2 * d)))
        _subnorm_store(o, h, sn_ref, o_ref, lam_init)


def _attn_body(qi_ref, kj_ref, q_ref, k_ref, vt_ref, bias_ref, lam_ref, sn_ref, o_ref, m_ref, l_ref, acc_ref,
               *, H, d, lam_init, kv_mult, n_tiles):
    p = pl.program_id(1)
    qi = qi_ref[p]
    kj = kj_ref[p]

    @pl.when(kj == 0)
    def _():
        m_ref[...] = jnp.full(m_ref.shape, -jnp.inf, F32)
        l_ref[...] = jnp.zeros(l_ref.shape, F32)
        acc_ref[...] = jnp.zeros(acc_ref.shape, F32)

    bidx = qi - kv_mult * kj

    def logits(c):
        return _dot_nt(k_ref[:, c * d:(c + 1) * d], q_ref[:, c * d:(c + 1) * d])

    def sweep(near):
        s_next = logits(0)
        for c in range(2 * H):
            h = c // 2
            s = s_next
            if c + 1 < 2 * H:
                s_next = logits(c + 1)
            if near:
                s = s + bias_ref[bidx, h]
            m_prev = m_ref[c]
            m_new = jnp.maximum(m_prev, jnp.max(s, axis=0, keepdims=True))
            alpha = jnp.exp2(m_prev - m_new)
            p = jnp.exp2(s - m_new[0:1])
            l_ref[c] = alpha * l_ref[c] + jnp.sum(p, axis=0, keepdims=True)
            vt = vt_ref[h * 2 * d:(h + 1) * 2 * d, :]
            acc_ref[c] = acc_ref[c] * alpha[0:1] + _dot(vt, p.astype(BF16))
            m_ref[c] = m_new

    @pl.when(bidx < n_tiles)
    def _():
        sweep(True)

    @pl.when(bidx >= n_tiles)
    def _():
        sweep(False)

    @pl.when(kj == qi // kv_mult)
    def _():
        lam = _diff_lambda(lam_ref, lam_init)
        for h in range(H):
            ot = (acc_ref[2 * h] / l_ref[2 * h][0:1] - lam * (acc_ref[2 * h + 1] / l_ref[2 * h + 1][0:1]))
            _subnorm_store(ot.T, h, sn_ref, o_ref, lam_init)


def _diff_attention_prompt(q, k, vt, rel_bias, lam_vecs, sub_norm, lam_init, *, nseq, d):
    T, QW = q.shape
    H = QW // (2 * d)
    L = T // nseq
    tq = _tile(L, 256, LANES)
    kv_mult = 2 if L % (2 * tq) == 0 else 1
    tk = kv_mult * tq
    assert tq % CHUNK == 0 and L % tq == 0
    nq, nk = L // tq, L // tk
    pairs = [(i, j) for i in range(nq) for j in range(i // kv_mult + 1)]
    qi = jnp.asarray([pq for pq, _ in pairs], jnp.int32)
    kj = jnp.asarray([pk for _, pk in pairs], jnp.int32)
    n_tiles = -(-(tk + MAX_DISTANCE - 1) // tq)
    bias = _bias_tiles(rel_bias, [(t * tq, 0) for t in range(n_tiles)], tq, tk, keys_first=True)
    qrow = lambda b, p, qi_r, kj_r: (b * nq + qi_r[p], 0)
    krow = lambda b, p, qi_r, kj_r: (b * nk + kj_r[p], 0)
    kcol = lambda b, p, qi_r, kj_r: (0, b * nk + kj_r[p])
    fixed2 = lambda b, p, qi_r, kj_r: (0, 0)
    grid_spec = pltpu.PrefetchScalarGridSpec(
        num_scalar_prefetch=2,
        grid=(nseq, len(pairs)),
        in_specs=[
            pl.BlockSpec((tq, QW), qrow),
            pl.BlockSpec((tk, QW), krow),
            pl.BlockSpec((QW, tk), kcol),
            pl.BlockSpec((n_tiles, H, tk, tq), lambda b, p, qi_r, kj_r: (0, 0, 0, 0),
                         pipeline_mode=pl.Buffered(1)),
            pl.BlockSpec((4, d), fixed2),
            pl.BlockSpec((1, 2 * d), fixed2),
        ],
        out_specs=pl.BlockSpec((tq, QW), qrow),
        scratch_shapes=[pltpu.VMEM((2 * H, F32_SUBLANES, tq), F32), pltpu.VMEM((2 * H, F32_SUBLANES, tq), F32),
                        pltpu.VMEM((2 * H, 2 * d, tq), F32)],
    )
    return pl.pallas_call(
        functools.partial(_attn_body, H=H, d=d, lam_init=lam_init, kv_mult=kv_mult, n_tiles=n_tiles),
        grid_spec=grid_spec,
        out_shape=jax.ShapeDtypeStruct((T, QW), BF16),
        compiler_params=_cparams(("parallel", "arbitrary")),
        name="diff_attention_prompt",
    )(qi, kj, q, k, vt, bias, lam_vecs, sub_norm.astype(F32).reshape(1, 2 * d))


def _attn_dec_body(q_ref, ck_ref, cv_ref, kn_ref, vn_ref, bc_ref, bn_ref, lam_ref, sn_ref, o_ref,
                   m_ref, l_ref, acc_ref, *, H, d, lam_init, n_blocks):
    kb_i = pl.program_id(1)

    @pl.when(kb_i == 0)
    def _():
        m_ref[...] = jnp.full(m_ref.shape, -jnp.inf, F32)
        l_ref[...] = jnp.zeros(l_ref.shape, F32)
        acc_ref[...] = jnp.zeros(acc_ref.shape, F32)

    def sweep(k_of, v_of, b_ref):
        for h in range(H):
            vb = v_of(h)
            for mp in range(2):
                c = 2 * h + mp
                s = _dot_nt(q_ref[:, c * d:(c + 1) * d], k_of(c)) + b_ref[h]
                _softmax_update(s, vb, m_ref, l_ref, acc_ref, c)

    tk = ck_ref.shape[1] // (2 * H)
    sweep(lambda c: ck_ref[0, pl.ds(c, tk, stride=2 * H), :].astype(BF16),
          lambda h: jnp.concatenate([cv_ref[0, pl.ds(half * H + h, tk, stride=2 * H), :] for half in range(2)],
                                    axis=1).astype(BF16), bc_ref)

    @pl.when(kb_i == n_blocks - 1)
    def _():
        sweep(lambda c: kn_ref[:, c * d:(c + 1) * d], lambda h: vn_ref[:, h * 2 * d:(h + 1) * 2 * d], bn_ref)
        _diff_finalize(lam_ref, sn_ref, o_ref, l_ref, acc_ref, H=H, d=d, lam_init=lam_init)


def _diff_attention_decode(q, k_new, v_new, cache_k, cache_v, rel_bias, lam_vecs, sub_norm, lam_init, *, d):
    B, P, H = cache_v.shape[:3]
    QW = H * 2 * d
    Lq = q.shape[0] // B
    tk = _tile(P, 512, LANES)
    nb = P // tk
    cache_v_rows = cache_v.reshape(B, P, H, 2, d).transpose(0, 1, 3, 2, 4).reshape(B, P * 2 * H, d)
    bias_c = _bias_tiles(rel_bias, [(P, j * tk) for j in range(nb)], Lq, tk)
    bias_n = _bias_tiles(rel_bias, [(P, P)], Lq, Lq)[0]
    seq = lambda b, j: (b, 0)
    fixed2 = lambda b, j: (0, 0)
    return pl.pallas_call(
        functools.partial(_attn_dec_body, H=H, d=d, lam_init=lam_init, n_blocks=nb),
        grid=(B, nb),
        in_specs=[
            pl.BlockSpec((Lq, QW), seq),
            pl.BlockSpec((1, tk * 2 * H, d), lambda b, j: (b, j, 0)),
            pl.BlockSpec((1, tk * 2 * H, d), lambda b, j: (b, j, 0)),
            pl.BlockSpec((Lq, QW), seq),
            pl.BlockSpec((Lq, QW), seq),
            pl.BlockSpec((None, H, Lq, tk), lambda b, j: (j, 0, 0, 0)),
            pl.BlockSpec((H, Lq, Lq), lambda b, j: (0, 0, 0)),
            pl.BlockSpec((4, d), fixed2),
            pl.BlockSpec((1, 2 * d), fixed2),
        ],
        out_specs=pl.BlockSpec((Lq, QW), seq),
        out_shape=jax.ShapeDtypeStruct((B * Lq, QW), BF16),
        scratch_shapes=[pltpu.VMEM((2 * H, Lq, LANES), F32), pltpu.VMEM((2 * H, Lq, LANES), F32),
                        pltpu.VMEM((2 * H, Lq, 2 * d), F32)],
        compiler_params=_cparams(("parallel", "arbitrary")),
        name="diff_attention_decode",
    )(q, cache_k.reshape(B, P * 2 * H, d), cache_v_rows, k_new, v_new, bias_c, bias_n,
      lam_vecs, sub_norm.astype(F32).reshape(1, 2 * d))


def _trunk(x, p, st_gdn, st_gdn_conv, st_ffn_conv, cache_k, cache_v):
    B, L, D = x.shape
    fresh = cache_k is None
    T = B * L
    depth = p['f_norm'].shape[0]
    n_a = p['a_norm'].shape[0]
    HV, dv = p['a_log'].shape[1], p['a_out_norm'].shape[1]
    VW = HV * dv
    QKVW = p['a_w_conv'].shape[2]
    dk = p['gdn_head_k']
    d = p['b_lam_q1'].shape[1]
    DFF = p['f_w_down'].shape[1]
    QW = p['b_w_q'].shape[2]

    time_major = not fresh
    if time_major:
        h = x.transpose(1, 0, 2).reshape(T, D)
        stride, nseq_conv = B, 1
    else:
        h = x.reshape(T, D)
        stride, nseq_conv = 1, B

    def to_seq(t):
        return t.reshape(L, B, -1).transpose(1, 0, 2).reshape(T, -1) if time_major else t

    def to_time(t):
        return t.reshape(B, L, -1).transpose(1, 0, 2).reshape(T, -1) if time_major else t

    def conv_state_in(st, width, chans):
        if st is None:
            return jnp.zeros((nseq_conv, (width - 1) * stride, chans), F32)
        return st.transpose(1, 0, 2).reshape(1, (width - 1) * B, chans)

    def conv_state_out(st, width):
        if time_major:
            return st.reshape(width - 1, B, -1).transpose(1, 0, 2)
        return st

    new_S, new_gconv, new_fconv = [], [], []
    k_f32 = v_f32 = k_bf = v_bf = None
    xn = _rmsnorm_cast(h, p['a_norm'][0]) if n_a > 0 else _rmsnorm_cast(h, p['b_norm'][0])
    xn_kv = None
    y = None
    for layer in range(depth):
        if layer < n_a:
            i = layer
            w_in = (i, p['a_w_in_bf'])
            w_ba = p['a_w_in'][i][:, QKVW + VW:]
            gw = p['a_w_conv'].shape[1]
            cst = conv_state_in(None if fresh else st_gdn_conv[i], gw, QKVW)
            qkv, gconv = _conv_matmul(xn, w_in, p['a_w_conv'][i], None, cst, groups=1, width=QKVW,
                                      stride=stride, nseq=nseq_conv, glu=False, name="gdn_qkv_conv",
                                      tm_pref=512, tn_pref=4096)
            (sz,) = _matmul(xn, w_in, lambda acc: (_silu(acc),), [BF16], "gdn_z_proj", col0=QKVW, width=VW)
            beta, g = _gate_proj(xn, w_ba[:, :HV], w_ba[:, HV:], p['a_log'][i], p['a_dt_bias'][i],
                                 stride=stride, chunk=min(CHUNK, L))
            S0 = jnp.zeros((B, HV, dk, dv), F32) if fresh else st_gdn[i]
            og, S = _gdn_core(to_seq(qkv), to_seq(sz), to_seq(beta), to_seq(g), S0, p['a_out_norm'][i],
                              nseq=B, chunk=min(CHUNK, L), hb=min(GDN_HEADS_PER_STEP, HV))
            new_S.append(S)
            new_gconv.append(conv_state_out(gconv, gw))
            h, xn = _matmul_residual(to_time(og), (i, p['a_w_out_bf']), h, [p['f_norm'][layer]],
                                     BF16, True, "gdn_out_proj")
        else:
            j = layer - n_a
            lam_init = 0.8 - 0.6 * math.exp(-0.3 * layer)
            scale = d ** -0.5 * LOG2_E
            (q,) = _matmul(xn, (j, p['b_w_q_bf']), lambda acc: (acc * scale,), [BF16], "diff_q_proj")
            lam_vecs = jnp.stack([p['b_lam_q1'][j], p['b_lam_k1'][j], p['b_lam_q2'][j],
                                  p['b_lam_k2'][j]]).astype(F32)
            if fresh:
                ao = _diff_attention_prompt(q, k_bf, v_bf, p['rel_bias'], lam_vecs, p['b_sub_norm'][j],
                                            lam_init, nseq=B, d=d)
            else:
                ao = _diff_attention_decode(to_seq(q), to_seq(k_bf), to_seq(v_bf),
                                            cache_k, cache_v, p['rel_bias'], lam_vecs, p['b_sub_norm'][j], lam_init, d=d)
                ao = to_time(ao)
            h, xn = _matmul_residual(ao, (j, p['b_w_o_bf']), h, [p['f_norm'][layer]],
                                     BF16, True, "diff_out_proj")
        fw = p['f_w_conv'].shape[1]
        fst = conv_state_in(None if fresh else st_ffn_conv[layer], fw, 2 * DFF)
        act, fconv = _conv_matmul(xn, (layer, p['f_w_up_bf']), p['f_w_conv'][layer],
                                  p['f_b_conv'][layer], fst, groups=2, width=2 * DFF, stride=stride,
                                  nseq=nseq_conv, glu=True, name="ffn_up_conv", tm_pref=512, tn_pref=2816)
        new_fconv.append(conv_state_out(fconv, fw))
        last = layer == depth - 1
        gains = []
        if layer == n_a - 1:
            gains.append(p['kv_norm'])
        if last:
            gains.append(p['final_norm'])
        elif layer + 1 < n_a:
            gains.append(p['a_norm'][layer + 1])
        else:
            gains.append(p['b_norm'][layer + 1 - n_a])
        outs = _matmul_residual(act, (layer, p['f_w_down_bf']), h, gains,
                                F32 if last else BF16, not last, "ffn_down_proj")
        if last:
            if layer == n_a - 1:
                xn_kv = outs[0].astype(BF16)
            y = outs[-1]
        else:
            h = outs[0]
            if layer == n_a - 1:
                xn_kv = outs[1]
            xn = outs[-1]
        if layer == n_a - 1:
            k_f32, k_bf = _head_projection(xn_kv, p['w_kv_bf'], 0, QW, d, "k_proj")
            v_f32, v_bf = _head_projection(xn_kv, p['w_kv_bf'], QW, QW, 2 * d, "v_proj", features_first=fresh)

    n_kh = 2 * (QW // (2 * d))
    y = to_seq(y).reshape(B, L, D)
    k_sh = to_seq(k_f32).reshape(B, L, n_kh, d)
    v_sh = to_seq(v_f32).reshape(B, L, n_kh // 2, 2 * d)
    return y, jnp.stack(new_S), jnp.stack(new_gconv), jnp.stack(new_fconv), k_sh, v_sh


def kernel(x_prompt, x_sample, state_gdn, state_gdn_conv, state_ffn_conv, cache_k, cache_v, a_norm, a_w_in, a_w_conv, a_log, a_dt_bias, a_out_norm, a_w_out, kv_norm, w_kv, b_norm, b_w_q, b_lam_q1, b_lam_k1, b_lam_q2, b_lam_k2, b_sub_norm, b_w_o, rel_bias, f_norm, f_w_up, f_w_conv, f_b_conv, f_w_down, final_norm):
    p = {
        'a_norm': a_norm, 'a_w_in': a_w_in, 'a_w_conv': a_w_conv, 'a_log': a_log,
        'a_dt_bias': a_dt_bias, 'a_out_norm': a_out_norm, 'a_w_out': a_w_out,
        'kv_norm': kv_norm, 'w_kv': w_kv,
        'b_norm': b_norm, 'b_w_q': b_w_q, 'b_lam_q1': b_lam_q1, 'b_lam_k1': b_lam_k1,
        'b_lam_q2': b_lam_q2, 'b_lam_k2': b_lam_k2, 'b_sub_norm': b_sub_norm, 'b_w_o': b_w_o,
        'rel_bias': rel_bias,
        'f_norm': f_norm, 'f_w_up': f_w_up, 'f_w_conv': f_w_conv, 'f_b_conv': f_b_conv,
        'f_w_down': f_w_down, 'final_norm': final_norm,
        'gdn_head_k': state_gdn.shape[3],
    }
    for name in ('a_w_out', 'w_kv', 'b_w_q', 'b_w_o', 'f_w_up', 'f_w_down'):
        p[name + '_bf'] = p[name].astype(BF16)
    n_main = a_w_conv.shape[2] + a_log.shape[1] * a_out_norm.shape[1]
    p['a_w_in_bf'] = a_w_in[:, :, :n_main].astype(BF16)
    out_p = _trunk(x_prompt, p, None, None, None, None, None)
    out_s = _trunk(x_sample, p, state_gdn, state_gdn_conv, state_ffn_conv, cache_k, cache_v)
    return (out_p[0], out_s[0]) + tuple(out_p[1:]) + tuple(out_s[1:])
```

```python
import functools
import math

import numpy as np
import jax
import jax.numpy as jnp
from jax import lax
from jax.experimental import pallas as pl
from jax.experimental.pallas import tpu as pltpu

F32 = jnp.float32
BF16 = jnp.bfloat16

CHUNK = 64
NORM_EPS = 1e-6
DIFF_SUBLN_EPS = 1e-5
L2_EPS = 1e-6
MAX_DISTANCE = 128
NEG_INF = -1e30
LOG2_E = math.log2(math.e)
GDN_HEADS_PER_STEP = 32

V7X_VMEM_LIMIT = 56 * 1024 * 1024
RESIDUAL_MATMUL_VMEM_BUDGET = 46 * 1024 * 1024
LANES = 128
V7X_MXU_COLS = 256
F32_SUBLANES = 8
BF16_SUBLANES = 16


def _cparams(sem):
    return pltpu.CompilerParams(dimension_semantics=sem, vmem_limit_bytes=V7X_VMEM_LIMIT)


def _tile(n, pref, mult):
    if n <= pref:
        return n
    t = (pref // mult) * mult
    while t >= mult:
        if n % t == 0:
            return t
        t -= mult
    return n


def _wspec(w, block, index_map, **kw):
    if isinstance(w, tuple):
        layer = w[0]
        return pl.BlockSpec((None,) + block, lambda *a: (layer,) + index_map(*a), **kw)
    return pl.BlockSpec(block, index_map, **kw)


def _warr(w):
    return w[1] if isinstance(w, tuple) else w


def _dot(a, b):
    return jnp.dot(a, b, preferred_element_type=F32)


def _dot_nt(a, b):
    return lax.dot_general(a, b, (((1,), (1,)), ((), ())), preferred_element_type=F32)


def _dot_tn(a, b):
    return lax.dot_general(a, b, (((0,), (0,)), ((), ())), preferred_element_type=F32)


def _dot_hi(a, b):
    return jnp.dot(a, b, preferred_element_type=F32, precision=lax.Precision.HIGHEST)


def _split_bf16(x):
    bits = lax.bitcast_convert_type(x, jnp.uint32) & jnp.uint32(0xFFFF0000)
    hi = lax.bitcast_convert_type(bits, F32)
    return hi, x - hi


def _x4_lhs(x, parts=None):
    xh, xl = parts or _split_bf16(x)
    xx = jnp.concatenate([xh, xl], axis=1).astype(BF16)
    return jnp.concatenate([xx, xx], axis=1)


def _x4_rhs(y, parts=None):
    yh, yl = parts or _split_bf16(y)
    return jnp.concatenate([yh, yh, yl, yl], axis=0).astype(BF16)


def _x4_both(x):
    parts = _split_bf16(x)
    return _x4_lhs(x, parts), _x4_rhs(x, parts)


def _dot_x4(lhs4, rhs4):
    return _dot(lhs4, rhs4)


def _silu(x):
    h = 0.5 * x
    return h + h * jnp.tanh(h)


def _rms_body(x_ref, g_ref, o_ref):
    x = x_ref[...]
    ms = jnp.mean(x * x, axis=-1, keepdims=True)
    o_ref[...] = (x * lax.rsqrt(ms + NORM_EPS) * g_ref[...]).astype(o_ref.dtype)


def _rmsnorm_cast(x, g):
    T, D = x.shape
    tm = _tile(T, 512, BF16_SUBLANES)
    return pl.pallas_call(
        _rms_body,
        grid=(T // tm,),
        in_specs=[pl.BlockSpec((tm, D), lambda i: (i, 0)), pl.BlockSpec((1, D), lambda i: (0, 0))],
        out_specs=pl.BlockSpec((tm, D), lambda i: (i, 0)),
        out_shape=jax.ShapeDtypeStruct((T, D), BF16),
        compiler_params=_cparams(("parallel",)),
        name="rmsnorm_cast",
    )(x, g.reshape(1, D).astype(F32))


def _mm_body(x_ref, w_ref, *o_refs, epilogue):
    acc = _dot(x_ref[...], w_ref[...])
    for o_ref, val in zip(o_refs, epilogue(acc)):
        o_ref[...] = val.astype(o_ref.dtype)


def _matmul(x, w, epilogue, out_dtypes, name, col0=0, width=None, tm_pref=1024, tn_pref=1024):
    T, K = x.shape
    N = _warr(w).shape[-1] if width is None else width
    tm = _tile(T, tm_pref, BF16_SUBLANES)
    tn = _tile(math.gcd(N, col0) if col0 else N, tn_pref, LANES)
    cb = col0 // tn
    outs = pl.pallas_call(
        functools.partial(_mm_body, epilogue=epilogue),
        grid=(N // tn, T // tm),
        in_specs=[pl.BlockSpec((tm, K), lambda n, m: (m, 0)), _wspec(w, (K, tn), lambda n, m: (0, n + cb))],
        out_specs=[pl.BlockSpec((tm, tn), lambda n, m: (m, n)) for _ in out_dtypes],
        out_shape=[jax.ShapeDtypeStruct((T, N), dt) for dt in out_dtypes],
        compiler_params=_cparams(("parallel", "parallel")),
        name=name,
    )(x, _warr(w))
    return outs


def _headproj_body(x_ref, w_ref, orow_ref, obf_ref, *, hd, features_first):
    acc = _dot(x_ref[...], w_ref[...])
    obf_ref[...] = (acc.T if features_first else acc).astype(obf_ref.dtype)
    tm = acc.shape[0]
    nh = acc.shape[1] // hd
    per_tok = acc.shape[1] // LANES
    for hh in range(nh):
        for part in range(hd // LANES):
            c0 = hh * hd + part * LANES
            orow_ref[pl.ds(part * nh + hh, tm, stride=per_tok), :] = acc[:, c0:c0 + LANES]


def _head_projection(x, w, col0, width, hd, name, features_first=False):
    T, K = x.shape
    nh = width // hd
    parts = hd // LANES
    tm = _tile(T, 512, LANES)
    cb = col0 // width
    if features_first:
        bf_spec = pl.BlockSpec((width, tm), lambda m: (0, m))
        bf_shape = jax.ShapeDtypeStruct((width, T), BF16)
    else:
        bf_spec = pl.BlockSpec((tm, width), lambda m: (m, 0))
        bf_shape = jax.ShapeDtypeStruct((T, width), BF16)
    per_tok = width // LANES
    rows, bf = pl.pallas_call(
        functools.partial(_headproj_body, hd=hd, features_first=features_first),
        grid=(T // tm,),
        in_specs=[pl.BlockSpec((tm, K), lambda m: (m, 0)),
                  pl.BlockSpec((K, width), lambda m: (0, cb), pipeline_mode=pl.Buffered(1))],
        out_specs=[pl.BlockSpec((tm * per_tok, LANES), lambda m: (m, 0)), bf_spec],
        out_shape=[jax.ShapeDtypeStruct((T * per_tok, LANES), F32), bf_shape],
        compiler_params=_cparams(("parallel",)),
        name=name,
    )(x, w)
    heads = rows.reshape(T, parts, nh, LANES).transpose(0, 2, 1, 3).reshape(T, nh, hd)
    return heads, bf


def _gate_body(x_ref, wb_ref, wa_ref, alog_ref, dtb_ref, beta_ref, gcum_ref, *, stride, chunk):
    x = x_ref[...]
    b = _dot(x, wb_ref[...])
    a = _dot(x, wa_ref[...]) + dtb_ref[...]
    beta_ref[...] = 1.0 / (1.0 + jnp.exp(-b))
    softplus = jnp.maximum(a, 0.0) + jnp.log(1.0 + jnp.exp(-jnp.abs(a)))
    g = -jnp.exp(alog_ref[...]) * softplus
    bs = stride * chunk
    ri = lax.broadcasted_iota(jnp.int32, (bs, bs), 0)
    ci = lax.broadcasted_iota(jnp.int32, (bs, bs), 1)
    earlier = jnp.logical_and(ci <= ri, (ri - ci) % stride == 0).astype(F32)
    for r0 in range(0, x.shape[0], bs):
        gcum_ref[r0:r0 + bs, :] = _dot_hi(earlier, g[r0:r0 + bs])


def _gate_proj(xn, w_b, w_a, a_log, dt_bias, *, stride, chunk):
    T, K = xn.shape
    H = w_b.shape[1]
    pad = LANES - H
    wb = jnp.pad(w_b, ((0, 0), (0, pad))).astype(BF16)
    wa = jnp.pad(w_a, ((0, 0), (0, pad))).astype(BF16)
    al = jnp.pad(a_log.astype(F32), (0, pad)).reshape(1, LANES)
    db = jnp.pad(dt_bias.astype(F32), (0, pad)).reshape(1, LANES)
    tm = _tile(T, 1024, stride * chunk)
    row = pl.BlockSpec((tm, K), lambda i: (i, 0))
    wsp = pl.BlockSpec((K, LANES), lambda i: (0, 0))
    vsp = pl.BlockSpec((1, LANES), lambda i: (0, 0))
    osp = pl.BlockSpec((tm, LANES), lambda i: (i, 0))
    beta, gcum = pl.pallas_call(
        functools.partial(_gate_body, stride=stride, chunk=chunk),
        grid=(T // tm,),
        in_specs=[row, wsp, wsp, vsp, vsp],
        out_specs=[osp, osp],
        out_shape=[jax.ShapeDtypeStruct((T, LANES), F32)] * 2,
        compiler_params=_cparams(("parallel",)),
        name="gdn_gate_proj",
    )(xn, wb, wa, al, db)
    return beta[:, :H], gcum[:, :H]


def _convmm_body(*refs, G, W, stride, tm, pad, tps, glu, n_sub):
    hist = (W - 1) * stride
    it = iter(refs)
    x_ref = next(it)
    w_refs = [next(it) for _ in range(G)]
    wc_refs = [next(it) for _ in range(G)]
    b_refs = [next(it) for _ in range(G)] if glu else None
    st_refs = [next(it) for _ in range(G)]
    out_ref = next(it)
    nst_refs = [next(it) for _ in range(G)]
    yscs = [[next(it) for _ in range(n_sub)] for _ in range(G)]

    m = pl.program_id(1)
    first = (m % tps) == 0
    last = (m % tps) == tps - 1
    sub = out_ref.shape[1] // n_sub

    @pl.when(first)
    def _():
        for g in range(G):
            for c in range(n_sub):
                yscs[g][c][pad - hist:pad, :] = st_refs[g][:, c * sub:(c + 1) * sub]

    @pl.when(jnp.logical_not(first))
    def _():
        for g in range(G):
            for c in range(n_sub):
                yscs[g][c][0:pad, :] = yscs[g][c][tm:tm + pad, :]

    def project(c):
        cols = slice(c * sub, (c + 1) * sub)
        for g in range(G):
            yscs[g][c][pad:pad + tm, :] = _dot(x_ref[...], w_refs[g][:, cols])

    def conv_act(c):
        cols = slice(c * sub, (c + 1) * sub)
        convs = []
        for g in range(G):
            acc = None
            for i in range(W):
                off = pad - (W - 1 - i) * stride
                term = yscs[g][c][off:off + tm, :] * wc_refs[g][i:i + 1, cols]
                acc = term if acc is None else acc + term
            if glu:
                acc = acc + b_refs[g][:, cols]
            convs.append(acc)
        res = _silu(convs[0]) * convs[1] if glu else _silu(convs[0])
        out_ref[:, cols] = res.astype(out_ref.dtype)

    project(0)
    for c in range(1, n_sub):
        project(c)
        conv_act(c - 1)
    conv_act(n_sub - 1)

    @pl.when(last)
    def _():
        for g in range(G):
            for c in range(n_sub):
                nst_refs[g][:, c * sub:(c + 1) * sub] = yscs[g][c][pad + tm - hist:pad + tm, :]


def _conv_matmul(x, w, wc, bias, state, *, groups, width, stride, nseq, glu, name, tm_pref, tn_pref):
    T, K = x.shape
    W = wc.shape[0]
    Ng = width // groups
    hist = (W - 1) * stride
    rows = T // nseq
    tm = _tile(rows, tm_pref, BF16_SUBLANES)
    stream_weights = T == tm
    tn = _tile(Ng, 2 * V7X_MXU_COLS if stream_weights else tn_pref, V7X_MXU_COLS)
    n_sub = max(tn // V7X_MXU_COLS, 1)
    tps = rows // tm
    pad = -(-hist // F32_SUBLANES) * F32_SUBLANES
    assert tm >= pad and state.shape == (nseq, hist, groups * Ng)
    nb = Ng // tn

    def col(g):
        return lambda n, m: (0, n + g * nb)

    def stcol(g):
        return lambda n, m: (m // tps, 0, n + g * nb)

    in_specs = [pl.BlockSpec((tm, K), lambda n, m: (m, 0))]
    args = [x]
    wmode = {} if stream_weights else {'pipeline_mode': pl.Buffered(1)}
    in_specs += [_wspec(w, (K, tn), col(g), **wmode) for g in range(groups)]
    args += [_warr(w)] * groups
    in_specs += [pl.BlockSpec((W, tn), col(g)) for g in range(groups)]
    args += [wc.astype(F32)] * groups
    if glu:
        in_specs += [pl.BlockSpec((1, tn), col(g)) for g in range(groups)]
        args += [bias.astype(F32).reshape(1, -1)] * groups
    in_specs += [pl.BlockSpec((None, hist, tn), stcol(g)) for g in range(groups)]
    args += [state.astype(F32)] * groups
    out_specs = [pl.BlockSpec((tm, tn), lambda n, m: (m, n))]
    out_specs += [pl.BlockSpec((None, hist, tn), stcol(0)) for g in range(groups)]
    out_shape = [jax.ShapeDtypeStruct((T, Ng), BF16)]
    out_shape += [jax.ShapeDtypeStruct((nseq, hist, Ng), F32) for g in range(groups)]
    outs = pl.pallas_call(
        functools.partial(_convmm_body, G=groups, W=W, stride=stride, tm=tm, pad=pad, tps=tps, glu=glu,
                          n_sub=n_sub),
        grid=(nb, T // tm),
        in_specs=in_specs,
        out_specs=out_specs,
        out_shape=out_shape,
        scratch_shapes=[pltpu.VMEM((pad + tm, tn // n_sub), F32) for _ in range(groups * n_sub)],
        compiler_params=_cparams(("parallel", "arbitrary")),
        name=name,
    )(*args)
    new_state = outs[1] if groups == 1 else jnp.concatenate(outs[1:], axis=-1)
    return outs[0], new_state


def _gdn_body(q_ref, k_ref, v_ref, z_ref, beta_ref, g_ref, s0_ref, gn_ref, o_ref, sout_ref, S,
              *, hb, rep, dk, dv, n_blocks, C):
    n = pl.program_id(2)

    @pl.when(n == 0)
    def _():
        S[...] = s0_ref[0]

    ri = lax.broadcasted_iota(jnp.int32, (C, C), 0)
    ci = lax.broadcasted_iota(jnp.int32, (C, C), 1)
    incl = ri >= ci
    strict = ri > ci
    eye = (ri == ci).astype(F32)
    gain = gn_ref[...]
    n_sq = int(round(math.log2(C))) - 1
    heads = range(hb)

    def prepare(r0):
        rows = slice(r0, r0 + C)
        Gc = g_ref[rows, :]
        beta = beta_ref[rows, :]
        kn, qn, kk, qk = [], [], [], []
        for jk in range(hb // rep):
            kf = k_ref[rows, jk * dk:(jk + 1) * dk].astype(F32)
            qf = q_ref[rows, jk * dk:(jk + 1) * dk].astype(F32)
            kn.append(kf * lax.rsqrt(jnp.sum(kf * kf, axis=-1, keepdims=True) + L2_EPS))
            qn.append(qf * lax.rsqrt(jnp.sum(qf * qf, axis=-1, keepdims=True) + L2_EPS) * (dk ** -0.5))
            kb = kn[jk].astype(BF16)
            kq = _dot_nt(jnp.concatenate([kb, qn[jk].astype(BF16)], axis=0), kb)
            kk.append(kq[:C])
            qk.append(kq[C:])
        Gcol = [Gc[:, j:j + 1] for j in heads]
        bcol = [beta[:, j:j + 1] for j in heads]
        dec_incl, Nm = [], []
        for j in heads:
            Grow = jnp.sum(eye * Gcol[j], axis=0, keepdims=True)
            diff = Gcol[j] - Grow
            dec = jnp.where(incl, jnp.exp(jnp.where(incl, diff, 0.0)), 0.0)
            dec_incl.append(dec)
            Nm.append(-(bcol[j] * kk[j // rep] * jnp.where(strict, dec, 0.0)))
        Tm = [eye + Nm[j] for j in heads]
        Np = [_dot_x4(*_x4_both(Nm[j])) for j in heads]
        for lvl in range(n_sq):
            nparts = [_split_bf16(Np[j]) for j in heads]
            rhs4 = [_x4_rhs(None, nparts[j]) for j in heads]
            tparts = [_split_bf16(Tm[j]) for j in heads]
            if lvl + 1 < n_sq:
                stacked = [tuple(jnp.concatenate([tp, npart], axis=0) for tp, npart in zip(tparts[j], nparts[j]))
                           for j in heads]
                res = [_dot_x4(_x4_lhs(None, stacked[j]), rhs4[j]) for j in heads]
                Tm = [Tm[j] + res[j][:C] for j in heads]
                Np = [res[j][C:] for j in heads]
            else:
                Tm = [Tm[j] + _dot_x4(_x4_lhs(None, tparts[j]), rhs4[j]) for j in heads]
        eG = [jnp.exp(Gcol[j]) for j in heads]
        sol = []
        for j in heads:
            vf = v_ref[rows, j * dv:(j + 1) * dv].astype(F32)
            rhs = jnp.concatenate([bcol[j] * vf, (bcol[j] * eG[j]) * kn[j // rep]], axis=1)
            sol.append(_dot_x4(_x4_lhs(Tm[j]), _x4_rhs(rhs)))
        wq, pk, u_v, decay = [], [], [], []
        for j in heads:
            Glast = Gc[C - 1:C, j:j + 1]
            q_g = qn[j // rep] * eG[j]
            wq.append(jnp.concatenate([sol[j][:, dv:].astype(BF16), q_g.astype(BF16)], axis=0))
            P = qk[j // rep] * dec_incl[j]
            k_d = kn[j // rep] * jnp.exp(Glast - Gcol[j])
            pk.append(jnp.concatenate([P.astype(BF16), k_d.T.astype(BF16)], axis=0))
            u_v.append(sol[j][:, :dv])
            decay.append(jnp.exp(Glast))
        return wq, pk, u_v, decay

    n_sub = q_ref.shape[0] // C
    prepared = [prepare(cc * C) for cc in range(n_sub)]
    Scur = [S[j] for j in heads]
    for cc in range(n_sub):
        rows = slice(cc * C, (cc + 1) * C)
        wq, pk, u_v, decay = prepared[cc]
        Sb = [Scur[j].astype(BF16) for j in heads]
        wqS = [_dot(wq[j], Sb[j]) for j in heads]
        ub = [(u_v[j] - wqS[j][:C]).astype(BF16) for j in heads]
        pku = [_dot(pk[j], ub[j]) for j in heads]
        Scur = [decay[j] * Scur[j] + pku[j][C:] for j in heads]
        for j in heads:
            o = wqS[j][C:] + pku[j][:C]
            on = o * lax.rsqrt(jnp.mean(o * o, axis=-1, keepdims=True) + NORM_EPS) * gain
            o_ref[rows, j * dv:(j + 1) * dv] = (
                on * z_ref[rows, j * dv:(j + 1) * dv].astype(F32)).astype(o_ref.dtype)
    for j in heads:
        S[j] = Scur[j]

    @pl.when(n == n_blocks - 1)
    def _():
        sout_ref[0] = S[...]


def _gdn_core(qkv, sz, beta, g, S0, out_norm, *, nseq, chunk, hb):
    T = qkv.shape[0]
    _, HV, dk, dv = S0.shape
    VW = HV * dv
    QK = (qkv.shape[1] - VW) // 2
    HK = QK // dk
    rep = HV // HK
    HG = HV // hb
    kb = hb // rep
    L = T // nseq
    n_chunks = L // chunk
    n_blocks, rows = n_chunks, chunk

    def group_major(t):
        t = t.reshape(T, HG, hb).transpose(1, 0, 2)
        return jnp.pad(t, ((0, 0), (0, 0), (0, LANES - hb)))

    rowblk = lambda b, h, n: b * n_blocks + n
    in_specs = [
        pl.BlockSpec((rows, kb * dk), lambda b, h, n: (rowblk(b, h, n), h)),
        pl.BlockSpec((rows, kb * dk), lambda b, h, n: (rowblk(b, h, n), HG + h)),
        pl.BlockSpec((rows, hb * dv), lambda b, h, n: (rowblk(b, h, n), 2 * QK // (hb * dv) + h)),
        pl.BlockSpec((rows, hb * dv), lambda b, h, n: (rowblk(b, h, n), h)),
        pl.BlockSpec((None, rows, LANES), lambda b, h, n: (h, rowblk(b, h, n), 0)),
        pl.BlockSpec((None, rows, LANES), lambda b, h, n: (h, rowblk(b, h, n), 0)),
        pl.BlockSpec((1, hb, dk, dv), lambda b, h, n: (b, h, 0, 0)),
        pl.BlockSpec((1, dv), lambda b, h, n: (0, 0)),
    ]
    out_specs = [
        pl.BlockSpec((rows, hb * dv), lambda b, h, n: (rowblk(b, h, n), h)),
        pl.BlockSpec((1, hb, dk, dv), lambda b, h, n: (b, h, 0, 0)),
    ]
    o, S = pl.pallas_call(
        functools.partial(_gdn_body, hb=hb, rep=rep, dk=dk, dv=dv, n_blocks=n_blocks, C=chunk),
        grid=(nseq, HG, n_blocks),
        in_specs=in_specs,
        out_specs=out_specs,
        out_shape=[jax.ShapeDtypeStruct((T, VW), BF16), jax.ShapeDtypeStruct(S0.shape, F32)],
        scratch_shapes=[pltpu.VMEM((hb, dk, dv), F32)],
        compiler_params=_cparams(("parallel", "parallel", "arbitrary")),
        name="gdn_core",
    )(qkv, qkv, qkv, sz, group_major(beta), group_major(g), S0.astype(F32),
      out_norm.astype(F32).reshape(1, dv))
    return o, S


def _mmres_body(x_ref, w_ref, res_ref, *refs, n_norm, with_h):
    gains = refs[:n_norm]
    outs = refs[n_norm:]
    h = res_ref[...] + _dot(x_ref[...], w_ref[...])
    k = 0
    if with_h:
        outs[0][...] = h
        k = 1
    if n_norm:
        inv = lax.rsqrt(jnp.mean(h * h, axis=-1, keepdims=True) + NORM_EPS)
        y = h * inv
        for i in range(n_norm):
            outs[k + i][...] = (y * gains[i][...]).astype(outs[k + i].dtype)


def _matmul_residual(x, w, res, gains, norm_dtype, with_h, name):
    T, K = x.shape
    D = _warr(w).shape[-1]
    n_f32 = (1 if with_h else 0) + (len(gains) if norm_dtype == F32 else 0)
    n_bf16 = len(gains) if norm_dtype == BF16 else 0

    def vmem_bytes(rows):
        return K * D * 2 + 2 * rows * (K * 2 + D * 4 + n_f32 * D * 4 + n_bf16 * D * 2)

    tm = next(t for t in (1024, 512, 256, 128, BF16_SUBLANES)
              if t == BF16_SUBLANES or vmem_bytes(t) <= RESIDUAL_MATMUL_VMEM_BUDGET)
    tm = _tile(T, tm, BF16_SUBLANES)
    row = lambda i: (i, 0)
    fixed = lambda i: (0, 0)
    n_norm = len(gains)
    in_specs = [pl.BlockSpec((tm, K), row),
                _wspec(w, (K, D), fixed, pipeline_mode=pl.Buffered(1)),
                pl.BlockSpec((tm, D), row)]
    in_specs += [pl.BlockSpec((1, D), fixed) for _ in gains]
    out_shape = ([jax.ShapeDtypeStruct((T, D), F32)] if with_h else []) \
        + [jax.ShapeDtypeStruct((T, D), norm_dtype) for _ in gains]
    outs = pl.pallas_call(
        functools.partial(_mmres_body, n_norm=n_norm, with_h=with_h),
        grid=(T // tm,),
        in_specs=in_specs,
        out_specs=[pl.BlockSpec((tm, D), row) for _ in out_shape],
        out_shape=out_shape,
        compiler_params=_cparams(("parallel",)),
        name=name,
    )(x, _warr(w), res, *[g.astype(F32).reshape(1, D) for g in gains])
    return outs


def _bucket_table(num_buckets):
    half = num_buckets // 2
    exact = half // 2
    n = np.arange(MAX_DISTANCE + 1)
    large = exact + (np.log(np.maximum(n, 1).astype(np.float32) / np.float32(exact))
                     / np.float32(math.log(MAX_DISTANCE / exact)) * np.float32(half - exact)).astype(np.int32)
    large = np.minimum(large, half - 1)
    return np.where(n < exact, n, large), half


def _bias_body(rb_ref, out_ref, *, tiles, steps, half, far_bucket, keys_first):
    h = pl.program_id(0)
    shape = out_ref.shape[2:]
    q_axis, k_axis = (1, 0) if keys_first else (0, 1)
    for t, (q0, k0) in enumerate(tiles):
        qpos = q0 + lax.broadcasted_iota(jnp.int32, shape, q_axis)
        kpos = k0 + lax.broadcasted_iota(jnp.int32, shape, k_axis)
        rel = kpos - qpos
        n = jnp.abs(rel)
        neg = jnp.full(shape, rb_ref[0, h], F32)
        pos = jnp.full(shape, rb_ref[half, h], F32)
        for thr, b in steps:
            ge = n >= thr
            neg = jnp.where(ge, rb_ref[b, h], neg)
            pos = jnp.where(ge, rb_ref[half + b, h], pos)
        bias = jnp.where(rel > 0, pos, neg) - rb_ref[far_bucket, h]
        visible = kpos // CHUNK <= qpos // CHUNK
        out_ref[t, 0] = jnp.where(visible, bias * LOG2_E, NEG_INF)


def _bias_tiles(rel_bias, tiles, nq, nk, keys_first=False):
    NB, H = rel_bias.shape
    table, half = _bucket_table(NB)
    steps = [(int(i), int(table[i])) for i in range(1, len(table)) if table[i] != table[i - 1]]
    shape = (nk, nq) if keys_first else (nq, nk)
    return pl.pallas_call(
        functools.partial(_bias_body, tiles=tuple(tiles), steps=tuple(steps), half=half,
                          far_bucket=int(table[-1]), keys_first=keys_first),
        grid=(H,),
        in_specs=[pl.BlockSpec(memory_space=pltpu.SMEM)],
        out_specs=pl.BlockSpec((len(tiles), 1) + shape, lambda h: (0, h, 0, 0)),
        out_shape=jax.ShapeDtypeStruct((len(tiles), H) + shape, F32),
        compiler_params=_cparams(("parallel",)),
        name="t5_bias_tiles",
    )(rel_bias.astype(F32))


def _lane_tile(x, n):
    if n <= LANES:
        return x[:, :n]
    return jnp.concatenate([x] * (n // LANES), axis=1)


def _softmax_update(s, vb, m_ref, l_ref, acc_ref, idx):
    m_prev = m_ref[idx]
    l_prev = l_ref[idx]
    m_new = jnp.maximum(m_prev, jnp.max(s, axis=1, keepdims=True))
    alpha = jnp.exp2(m_prev - m_new)
    p = jnp.exp2(s - _lane_tile(m_new, s.shape[1]))
    l_ref[idx] = alpha * l_prev + jnp.sum(p, axis=1, keepdims=True)
    acc_ref[idx] = acc_ref[idx] * _lane_tile(alpha, vb.shape[1]) + _dot(p.astype(BF16), vb)
    m_ref[idx] = m_new


def _diff_lambda(lam_ref, lam_init):
    lv = lam_ref[...]
    return (jnp.exp(jnp.sum(lv[0:1] * lv[1:2], axis=1, keepdims=True))
            - jnp.exp(jnp.sum(lv[2:3] * lv[3:4], axis=1, keepdims=True)) + lam_init)


def _subnorm_store(o, h, sn_ref, o_ref, lam_init):
    w = o.shape[1]
    on = o * lax.rsqrt(jnp.mean(o * o, axis=-1, keepdims=True) + DIFF_SUBLN_EPS) * sn_ref[...]
    o_ref[:, h * w:(h + 1) * w] = (on * (1.0 - lam_init)).astype(o_ref.dtype)


def _diff_finalize(lam_ref, sn_ref, o_ref, l_ref, acc_ref, *, H, d, lam_init):
    lam = _diff_lambda(lam_ref, lam_init)
    for h in range(H):
        o = (acc_ref[2 * h] / _lane_tile(l_ref[2 * h], 2 * d)
             - lam * (acc_ref[2 * h + 1] / _lane_tile(l_ref[2 * h + 1], 2 * d)))
        _subnorm_store(o, h, sn_ref, o_ref, lam_init)


def _attn_body(qi_ref, kj_ref, q_ref, k_ref, vt_ref, bias_ref, lam_ref, sn_ref, o_ref, m_ref, l_ref, acc_ref,
               *, H, d, lam_init, kv_mult, n_tiles):
    p = pl.program_id(1)
    qi = qi_ref[p]
    kj = kj_ref[p]

    @pl.when(kj == 0)
    def _():
        m_ref[...] = jnp.full(m_ref.shape, -jnp.inf, F32)
        l_ref[...] = jnp.zeros(l_ref.shape, F32)
        acc_ref[...] = jnp.zeros(acc_ref.shape, F32)

    bidx = qi - kv_mult * kj

    def logits(c):
        return _dot_nt(k_ref[:, c * d:(c + 1) * d], q_ref[:, c * d:(c + 1) * d])

    def sweep(near):
        s_next = logits(0)
        for c in range(2 * H):
            h = c // 2
            s = s_next
            if c + 1 < 2 * H:
                s_next = logits(c + 1)
            if near:
                s = s + bias_ref[bidx, h]
            m_prev = m_ref[c]
            m_new = jnp.maximum(m_prev, jnp.max(s, axis=0, keepdims=True))
            alpha = jnp.exp2(m_prev - m_new)
            p = jnp.exp2(s - m_new[0:1])
            l_ref[c] = alpha * l_ref[c] + jnp.sum(p, axis=0, keepdims=True)
            vt = vt_ref[h * 2 * d:(h + 1) * 2 * d, :]
            acc_ref[c] = acc_ref[c] * alpha[0:1] + _dot(vt, p.astype(BF16))
            m_ref[c] = m_new

    @pl.when(bidx < n_tiles)
    def _():
        sweep(True)

    @pl.when(bidx >= n_tiles)
    def _():
        sweep(False)

    @pl.when(kj == qi // kv_mult)
    def _():
        lam = _diff_lambda(lam_ref, lam_init)
        for h in range(H):
            ot = (acc_ref[2 * h] / l_ref[2 * h][0:1] - lam * (acc_ref[2 * h + 1] / l_ref[2 * h + 1][0:1]))
            _subnorm_store(ot.T, h, sn_ref, o_ref, lam_init)


def _diff_attention_prompt(q, k, vt, rel_bias, lam_vecs, sub_norm, lam_init, *, nseq, d):
    T, QW = q.shape
    H = QW // (2 * d)
    L = T // nseq
    tq = _tile(L, 256, LANES)
    kv_mult = 2 if L % (2 * tq) == 0 else 1
    tk = kv_mult * tq
    assert tq % CHUNK == 0 and L % tq == 0
    nq, nk = L // tq, L // tk
    pairs = [(i, j) for i in range(nq) for j in range(i // kv_mult + 1)]
    qi = jnp.asarray([pq for pq, _ in pairs], jnp.int32)
    kj = jnp.asarray([pk for _, pk in pairs], jnp.int32)
    n_tiles = -(-(tk + MAX_DISTANCE - 1) // tq)
    bias = _bias_tiles(rel_bias, [(t * tq, 0) for t in range(n_tiles)], tq, tk, keys_first=True)
    qrow = lambda b, p, qi_r, kj_r: (b * nq + qi_r[p], 0)
    krow = lambda b, p, qi_r, kj_r: (b * nk + kj_r[p], 0)
    kcol = lambda b, p, qi_r, kj_r: (0, b * nk + kj_r[p])
    fixed2 = lambda b, p, qi_r, kj_r: (0, 0)
    grid_spec = pltpu.PrefetchScalarGridSpec(
        num_scalar_prefetch=2,
        grid=(nseq, len(pairs)),
        in_specs=[
            pl.BlockSpec((tq, QW), qrow),
            pl.BlockSpec((tk, QW), krow),
            pl.BlockSpec((QW, tk), kcol),
            pl.BlockSpec((n_tiles, H, tk, tq), lambda b, p, qi_r, kj_r: (0, 0, 0, 0),
                         pipeline_mode=pl.Buffered(1)),
            pl.BlockSpec((4, d), fixed2),
            pl.BlockSpec((1, 2 * d), fixed2),
        ],
        out_specs=pl.BlockSpec((tq, QW), qrow),
        scratch_shapes=[pltpu.VMEM((2 * H, F32_SUBLANES, tq), F32), pltpu.VMEM((2 * H, F32_SUBLANES, tq), F32),
                        pltpu.VMEM((2 * H, 2 * d, tq), F32)],
    )
    return pl.pallas_call(
        functools.partial(_attn_body, H=H, d=d, lam_init=lam_init, kv_mult=kv_mult, n_tiles=n_tiles),
        grid_spec=grid_spec,
        out_shape=jax.ShapeDtypeStruct((T, QW), BF16),
        compiler_params=_cparams(("parallel", "arbitrary")),
        name="diff_attention_prompt",
    )(qi, kj, q, k, vt, bias, lam_vecs, sub_norm.astype(F32).reshape(1, 2 * d))


def _attn_dec_body(q_ref, ck_ref, cv_ref, kn_ref, vn_ref, bc_ref, bn_ref, lam_ref, sn_ref, o_ref,
                   m_ref, l_ref, acc_ref, *, H, d, lam_init, n_blocks):
    kb_i = pl.program_id(1)

    @pl.when(kb_i == 0)
    def _():
        m_ref[...] = jnp.full(m_ref.shape, -jnp.inf, F32)
        l_ref[...] = jnp.zeros(l_ref.shape, F32)
        acc_ref[...] = jnp.zeros(acc_ref.shape, F32)

    def sweep(k_of, v_of, b_ref):
        for h in range(H):
            vb = v_of(h)
            for mp in range(2):
                c = 2 * h + mp
                s = _dot_nt(q_ref[:, c * d:(c + 1) * d], k_of(c)) + b_ref[h]
                _softmax_update(s, vb, m_ref, l_ref, acc_ref, c)

    tk = ck_ref.shape[1] // (2 * H)
    sweep(lambda c: ck_ref[0, pl.ds(c, tk, stride=2 * H), :].astype(BF16),
          lambda h: jnp.concatenate([cv_ref[0, pl.ds(half * H + h, tk, stride=2 * H), :] for half in range(2)],
                                    axis=1).astype(BF16), bc_ref)

    @pl.when(kb_i == n_blocks - 1)
    def _():
        sweep(lambda c: kn_ref[:, c * d:(c + 1) * d], lambda h: vn_ref[:, h * 2 * d:(h + 1) * 2 * d], bn_ref)
        _diff_finalize(lam_ref, sn_ref, o_ref, l_ref, acc_ref, H=H, d=d, lam_init=lam_init)


def _diff_attention_decode(q, k_new, v_new, cache_k, cache_v, rel_bias, lam_vecs, sub_norm, lam_init, *, d):
    B, P, H = cache_v.shape[:3]
    QW = H * 2 * d
    Lq = q.shape[0] // B
    tk = _tile(P, 512, LANES)
    nb = P // tk
    cache_v_rows = cache_v.reshape(B, P, H, 2, d).transpose(0, 1, 3, 2, 4).reshape(B, P * 2 * H, d)
    bias_c = _bias_tiles(rel_bias, [(P, j * tk) for j in range(nb)], Lq, tk)
    bias_n = _bias_tiles(rel_bias, [(P, P)], Lq, Lq)[0]
    seq = lambda b, j: (b, 0)
    fixed2 = lambda b, j: (0, 0)
    return pl.pallas_call(
        functools.partial(_attn_dec_body, H=H, d=d, lam_init=lam_init, n_blocks=nb),
        grid=(B, nb),
        in_specs=[
            pl.BlockSpec((Lq, QW), seq),
            pl.BlockSpec((1, tk * 2 * H, d), lambda b, j: (b, j, 0)),
            pl.BlockSpec((1, tk * 2 * H, d), lambda b, j: (b, j, 0)),
            pl.BlockSpec((Lq, QW), seq),
            pl.BlockSpec((Lq, QW), seq),
            pl.BlockSpec((None, H, Lq, tk), lambda b, j: (j, 0, 0, 0)),
            pl.BlockSpec((H, Lq, Lq), lambda b, j: (0, 0, 0)),
            pl.BlockSpec((4, d), fixed2),
            pl.BlockSpec((1, 2 * d), fixed2),
        ],
        out_specs=pl.BlockSpec((Lq, QW), seq),
        out_shape=jax.ShapeDtypeStruct((B * Lq, QW), BF16),
        scratch_shapes=[pltpu.VMEM((2 * H, Lq, LANES), F32), pltpu.VMEM((2 * H, Lq, LANES), F32),
                        pltpu.VMEM((2 * H, Lq, 2 * d), F32)],
        compiler_params=_cparams(("parallel", "arbitrary")),
        name="diff_attention_decode",
    )(q, cache_k.reshape(B, P * 2 * H, d), cache_v_rows, k_new, v_new, bias_c, bias_n,
      lam_vecs, sub_norm.astype(F32).reshape(1, 2 * d))


def _trunk(x, p, st_gdn, st_gdn_conv, st_ffn_conv, cache_k, cache_v):
    B, L, D = x.shape
    fresh = cache_k is None
    T = B * L
    depth = p['f_norm'].shape[0]
    n_a = p['a_norm'].shape[0]
    HV, dv = p['a_log'].shape[1], p['a_out_norm'].shape[1]
    VW = HV * dv
    QKVW = p['a_w_conv'].shape[2]
    dk = p['gdn_head_k']
    d = p['b_lam_q1'].shape[1]
    DFF = p['f_w_down'].shape[1]
    QW = p['b_w_q'].shape[2]

    time_major = not fresh
    if time_major:
        h = x.transpose(1, 0, 2).reshape(T, D)
        stride, nseq_conv = B, 1
    else:
        h = x.reshape(T, D)
        stride, nseq_conv = 1, B

    def to_seq(t):
        return t.reshape(L, B, -1).transpose(1, 0, 2).reshape(T, -1) if time_major else t

    def to_time(t):
        return t.reshape(B, L, -1).transpose(1, 0, 2).reshape(T, -1) if time_major else t

    def conv_state_in(st, width, chans):
        if st is None:
            return jnp.zeros((nseq_conv, (width - 1) * stride, chans), F32)
        return st.transpose(1, 0, 2).reshape(1, (width - 1) * B, chans)

    def conv_state_out(st, width):
        if time_major:
            return st.reshape(width - 1, B, -1).transpose(1, 0, 2)
        return st

    new_S, new_gconv, new_fconv = [], [], []
    k_f32 = v_f32 = k_bf = v_bf = None
    xn = _rmsnorm_cast(h, p['a_norm'][0]) if n_a > 0 else _rmsnorm_cast(h, p['b_norm'][0])
    xn_kv = None
    y = None
    for layer in range(depth):
        if layer < n_a:
            i = layer
            w_in = (i, p['a_w_in_bf'])
            w_ba = p['a_w_in'][i][:, QKVW + VW:]
            gw = p['a_w_conv'].shape[1]
            cst = conv_state_in(None if fresh else st_gdn_conv[i], gw, QKVW)
            qkv, gconv = _conv_matmul(xn, w_in, p['a_w_conv'][i], None, cst, groups=1, width=QKVW,
                                      stride=stride, nseq=nseq_conv, glu=False, name="gdn_qkv_conv",
                                      tm_pref=512, tn_pref=4096)
            (sz,) = _matmul(xn, w_in, lambda acc: (_silu(acc),), [BF16], "gdn_z_proj", col0=QKVW, width=VW)
            beta, g = _gate_proj(xn, w_ba[:, :HV], w_ba[:, HV:], p['a_log'][i], p['a_dt_bias'][i],
                                 stride=stride, chunk=min(CHUNK, L))
            S0 = jnp.zeros((B, HV, dk, dv), F32) if fresh else st_gdn[i]
            og, S = _gdn_core(to_seq(qkv), to_seq(sz), to_seq(beta), to_seq(g), S0, p['a_out_norm'][i],
                              nseq=B, chunk=min(CHUNK, L), hb=min(GDN_HEADS_PER_STEP, HV))
            new_S.append(S)
            new_gconv.append(conv_state_out(gconv, gw))
            h, xn = _matmul_residual(to_time(og), (i, p['a_w_out_bf']), h, [p['f_norm'][layer]],
                                     BF16, True, "gdn_out_proj")
        else:
            j = layer - n_a
            lam_init = 0.8 - 0.6 * math.exp(-0.3 * layer)
            scale = d ** -0.5 * LOG2_E
            (q,) = _matmul(xn, (j, p['b_w_q_bf']), lambda acc: (acc * scale,), [BF16], "diff_q_proj")
            lam_vecs = jnp.stack([p['b_lam_q1'][j], p['b_lam_k1'][j], p['b_lam_q2'][j],
                                  p['b_lam_k2'][j]]).astype(F32)
            if fresh:
                ao = _diff_attention_prompt(q, k_bf, v_bf, p['rel_bias'], lam_vecs, p['b_sub_norm'][j],
                                            lam_init, nseq=B, d=d)
            else:
                ao = _diff_attention_decode(to_seq(q), to_seq(k_bf), to_seq(v_bf),
                                            cache_k, cache_v, p['rel_bias'], lam_vecs, p['b_sub_norm'][j], lam_init, d=d)
                ao = to_time(ao)
            h, xn = _matmul_residual(ao, (j, p['b_w_o_bf']), h, [p['f_norm'][layer]],
                                     BF16, True, "diff_out_proj")
        fw = p['f_w_conv'].shape[1]
        fst = conv_state_in(None if fresh else st_ffn_conv[layer], fw, 2 * DFF)
        act, fconv = _conv_matmul(xn, (layer, p['f_w_up_bf']), p['f_w_conv'][layer],
                                  p['f_b_conv'][layer], fst, groups=2, width=2 * DFF, stride=stride,
                                  nseq=nseq_conv, glu=True, name="ffn_up_conv", tm_pref=512, tn_pref=2816)
        new_fconv.append(conv_state_out(fconv, fw))
        last = layer == depth - 1
        gains = []
        if layer == n_a - 1:
            gains.append(p['kv_norm'])
        if last:
            gains.append(p['final_norm'])
        elif layer + 1 < n_a:
            gains.append(p['a_norm'][layer + 1])
        else:
            gains.append(p['b_norm'][layer + 1 - n_a])
        outs = _matmul_residual(act, (layer, p['f_w_down_bf']), h, gains,
                                F32 if last else BF16, not last, "ffn_down_proj")
        if last:
            if layer == n_a - 1:
                xn_kv = outs[0].astype(BF16)
            y = outs[-1]
        else:
            h = outs[0]
            if layer == n_a - 1:
                xn_kv = outs[1]
            xn = outs[-1]
        if layer == n_a - 1:
            k_f32, k_bf = _head_projection(xn_kv, p['w_kv_bf'], 0, QW, d, "k_proj")
            v_f32, v_bf = _head_projection(xn_kv, p['w_kv_bf'], QW, QW, 2 * d, "v_proj", features_first=fresh)

    n_kh = 2 * (QW // (2 * d))
    y = to_seq(y).reshape(B, L, D)
    k_sh = to_seq(k_f32).reshape(B, L, n_kh, d)
    v_sh = to_seq(v_f32).reshape(B, L, n_kh // 2, 2 * d)
    return y, jnp.stack(new_S), jnp.stack(new_gconv), jnp.stack(new_fconv), k_sh, v_sh


def kernel(x_prompt, x_sample, state_gdn, state_gdn_conv, state_ffn_conv, cache_k, cache_v, a_norm, a_w_in, a_w_conv, a_log, a_dt_bias, a_out_norm, a_w_out, kv_norm, w_kv, b_norm, b_w_q, b_lam_q1, b_lam_k1, b_lam_q2, b_lam_k2, b_sub_norm, b_w_o, rel_bias, f_norm, f_w_up, f_w_conv, f_b_conv, f_w_down, final_norm):
    p = {
        'a_norm': a_norm, 'a_w_in': a_w_in, 'a_w_conv': a_w_conv, 'a_log': a_log,
        'a_dt_bias': a_dt_bias, 'a_out_norm': a_out_norm, 'a_w_out': a_w_out,
        'kv_norm': kv_norm, 'w_kv': w_kv,
        'b_norm': b_norm, 'b_w_q': b_w_q, 'b_lam_q1': b_lam_q1, 'b_lam_k1': b_lam_k1,
        'b_lam_q2': b_lam_q2, 'b_lam_k2': b_lam_k2, 'b_sub_norm': b_sub_norm, 'b_w_o': b_w_o,
        'rel_bias': rel_bias,
        'f_norm': f_norm, 'f_w_up': f_w_up, 'f_w_conv': f_w_conv, 'f_b_conv': f_b_conv,
        'f_w_down': f_w_down, 'final_norm': final_norm,
        'gdn_head_k': state_gdn.shape[3],
    }
    for name in ('a_w_out', 'w_kv', 'b_w_q', 'b_w_o', 'f_w_up', 'f_w_down'):
        p[name + '_bf'] = p[name].astype(BF16)
    n_main = a_w_conv.shape[2] + a_log.shape[1] * a_out_norm.shape[1]
    p['a_w_in_bf'] = a_w_in[:, :, :n_main].astype(BF16)
    out_p = _trunk(x_prompt, p, None, None, None, None, None)
    out_s = _trunk(x_sample, p, state_gdn, state_gdn_conv, state_ffn_conv, cache_k, cache_v)
    return (out_p[0], out_s[0]) + tuple(out_p[1:]) + tuple(out_s[1:])
```
